```python
import math
import jax, jax.numpy as jnp
from jax import lax
import numpy as np

D_MODEL = 1024
BATCH = 8
SEQ = 4096
DEPTH = 4

N_A_LAYERS = DEPTH // 2
N_B_LAYERS = DEPTH - N_A_LAYERS
CONV_WIDTH = 31
HEAD_DIM = 64
N_HEADS = D_MODEL // HEAD_DIM
N_KV_HEADS = N_HEADS // 4
GROUP = N_HEADS // N_KV_HEADS
WINDOW = 128
BLOCK = 128
N_BUCKETS = 32
MAX_DISTANCE = 128
D_FF = -(-8 * D_MODEL // (3 * 256)) * 256
EPS = 1e-6
NEG_INF = -1e30

kernel_name = "yoco_conformer_swa_sink_hybrid"


def rmsnorm(x, g):
    xf = x.astype(jnp.float32)
    xf = xf * lax.rsqrt(jnp.mean(xf * xf, axis=-1, keepdims=True) + EPS)
    return (xf * g.astype(jnp.float32)).astype(x.dtype)


def layernorm(x, g, b):
    xf = x.astype(jnp.float32)
    mu = jnp.mean(xf, axis=-1, keepdims=True)
    var = jnp.mean(jnp.square(xf - mu), axis=-1, keepdims=True)
    y = (xf - mu) * lax.rsqrt(var + EPS) * g.astype(jnp.float32) + b.astype(jnp.float32)
    return y.astype(x.dtype)


def swiglu_ffn(x, w_up, w_down):
    gate, up = jnp.split(x @ w_up, 2, axis=-1)
    return (jax.nn.silu(gate) * up) @ w_down


def conformer_conv(x, w_pw1, b_pw1, w_dw, b_dw, ln_g, ln_b, w_pw2, b_pw2):
    a = jax.nn.glu(x @ w_pw1 + b_pw1, axis=-1)
    y = lax.conv_general_dilated(
        a, w_dw[:, None, :].astype(a.dtype), window_strides=(1,),
        padding=((CONV_WIDTH - 1, 0),),
        dimension_numbers=('NWC', 'WIO', 'NWC'),
        feature_group_count=D_MODEL) + b_dw
    y = jax.nn.silu(layernorm(y, ln_g, ln_b))
    return y @ w_pw2 + b_pw2


def t5_causal_bucket(dist):
    max_exact = N_BUCKETS // 2
    d = jnp.maximum(dist, 0)
    log_ratio = jnp.log(jnp.maximum(d, 1).astype(jnp.float32) / max_exact) / math.log(MAX_DISTANCE / max_exact)
    large = max_exact + (log_ratio * (N_BUCKETS - max_exact)).astype(jnp.int32)
    large = jnp.minimum(large, N_BUCKETS - 1)
    return jnp.where(d < max_exact, d, large)


def banded_sink_attention(q, k, v, sinks, rel_bias):
    B, S = q.shape[0], q.shape[1]
    nb = S // BLOCK
    qb = q.reshape(B, nb, BLOCK, N_KV_HEADS, GROUP, HEAD_DIM)
    kb = k.reshape(B, nb, BLOCK, N_KV_HEADS, HEAD_DIM)
    vb = v.reshape(B, nb, BLOCK, N_KV_HEADS, HEAD_DIM)
    pad = ((0, 0), (1, 0), (0, 0), (0, 0), (0, 0))
    k_band = jnp.concatenate([jnp.pad(kb, pad)[:, :-1], kb], axis=2)
    v_band = jnp.concatenate([jnp.pad(vb, pad)[:, :-1], vb], axis=2)

    s = jnp.einsum('bnqhgd,bnkhd->bnhgqk', qb, k_band,
                   preferred_element_type=jnp.float32) * (HEAD_DIM ** -0.5)

    qi = jnp.arange(BLOCK, dtype=jnp.int32)
    kj = jnp.arange(2 * BLOCK, dtype=jnp.int32)
    dist = qi[:, None] + BLOCK - kj[None, :]
    in_window = (dist >= 0) & (dist < WINDOW)
    bias = rel_bias.astype(jnp.float32)[t5_causal_bucket(dist)]
    bias = jnp.transpose(bias, (2, 0, 1)).reshape(N_KV_HEADS, GROUP, BLOCK, 2 * BLOCK)
    key_pos = (jnp.arange(nb, dtype=jnp.int32)[:, None] - 1) * BLOCK + kj[None, :]
    mask = in_window[None, :, :] & (key_pos >= 0)[:, None, :]

    s = jnp.where(mask[None, :, None, None], s + bias, NEG_INF)
    sink = sinks.astype(jnp.float32).reshape(N_KV_HEADS, GROUP, 1, 1)
    m = jnp.maximum(jnp.max(s, axis=-1, keepdims=True), sink)
    p = jnp.exp(s - m)
    probs = p / (jnp.sum(p, axis=-1, keepdims=True) + jnp.exp(sink - m))
    o = jnp.einsum('bnhgqk,bnkhd->bnqhgd', probs.astype(v.dtype), v_band)
    return o.reshape(B, S, N_HEADS * HEAD_DIM)


def setup_inputs(seed: int = 0) -> dict:
    key = jax.random.key(seed)
    ks = jax.random.split(key, 24)
    f32 = jnp.float32
    D, HD, KVD = D_MODEL, N_HEADS * HEAD_DIM, N_KV_HEADS * HEAD_DIM
    nrm = lambda k, shape, scale: jax.random.normal(k, shape, f32) * scale
    gain = lambda k, shape: 1.0 + 0.05 * jax.random.normal(k, shape, f32)
    return {
        "x": jax.random.normal(ks[0], (BATCH, SEQ, D), f32),
        "norm_mix": gain(ks[1], (DEPTH, D)),
        "norm_ffn": gain(ks[2], (DEPTH, D)),
        "conv_w_pw1": nrm(ks[3], (N_A_LAYERS, D, 2 * D), D ** -0.5),
        "conv_b_pw1": nrm(ks[4], (N_A_LAYERS, 2 * D), 0.02),
        "conv_w_dw": nrm(ks[5], (N_A_LAYERS, CONV_WIDTH, D), CONV_WIDTH ** -0.5),
        "conv_b_dw": nrm(ks[6], (N_A_LAYERS, D), 0.02),
        "conv_ln_g": gain(ks[7], (N_A_LAYERS, D)),
        "conv_ln_b": nrm(ks[8], (N_A_LAYERS, D), 0.02),
        "conv_w_pw2": nrm(ks[9], (N_A_LAYERS, D, D), D ** -0.5),
        "conv_b_pw2": nrm(ks[10], (N_A_LAYERS, D), 0.02),
        "norm_kv": gain(ks[11], (D,)),
        "w_kv": nrm(ks[12], (D, 2 * KVD), D ** -0.5),
        "w_q": nrm(ks[13], (N_B_LAYERS, D, HD), D ** -0.5),
        "w_o": nrm(ks[14], (N_B_LAYERS, HD, D), HD ** -0.5),
        "sinks": nrm(ks[15], (N_B_LAYERS, N_HEADS), 0.5),
        "rel_bias": nrm(ks[16], (N_BUCKETS, N_HEADS), 0.5),
        "ffn_w_up": nrm(ks[17], (DEPTH, D, 2 * D_FF), D ** -0.5),
        "ffn_w_down": nrm(ks[18], (DEPTH, D_FF, D), D_FF ** -0.5),
        "norm_final": gain(ks[19], (D,)),
    }


def reference(x, norm_mix, norm_ffn, conv_w_pw1, conv_b_pw1, conv_w_dw, conv_b_dw,
              conv_ln_g, conv_ln_b, conv_w_pw2, conv_b_pw2, norm_kv, w_kv, w_q, w_o,
              sinks, rel_bias, ffn_w_up, ffn_w_down, norm_final):
    B, S = x.shape[0], x.shape[1]
    h = x
    k_shared = v_shared = None
    for l in range(DEPTH):
        if l < N_A_LAYERS:
            i = l
            h = h + conformer_conv(rmsnorm(h, norm_mix[l]), conv_w_pw1[i], conv_b_pw1[i],
                                   conv_w_dw[i], conv_b_dw[i], conv_ln_g[i], conv_ln_b[i],
                                   conv_w_pw2[i], conv_b_pw2[i])
        else:
            if l == N_A_LAYERS:
                kv = rmsnorm(h, norm_kv) @ w_kv
                k_flat, v_flat = jnp.split(kv, 2, axis=-1)
                k_shared = k_flat.reshape(B, S, N_KV_HEADS, HEAD_DIM)
                v_shared = v_flat.reshape(B, S, N_KV_HEADS, HEAD_DIM)
            j = l - N_A_LAYERS
            q = (rmsnorm(h, norm_mix[l]) @ w_q[j]).reshape(B, S, N_HEADS, HEAD_DIM)
            attn = banded_sink_attention(q, k_shared, v_shared, sinks[j], rel_bias)
            h = h + attn @ w_o[j]
        h = h + swiglu_ffn(rmsnorm(h, norm_ffn[l]), ffn_w_up[l], ffn_w_down[l])
    return rmsnorm(h, norm_final)
```

```python
import functools
import math

import jax
import jax.numpy as jnp
import numpy as np
from jax import lax
from jax.experimental import pallas as pl
from jax.experimental.pallas import tpu as pltpu

D_MODEL = 1024
DEPTH = 4
N_A_LAYERS = DEPTH // 2
CONV_WIDTH = 31
HEAD_DIM = 64
N_HEADS = D_MODEL // HEAD_DIM
N_KV_HEADS = N_HEADS // 4
GROUP = N_HEADS // N_KV_HEADS
WINDOW = 128
BLOCK = 128
N_BUCKETS = 32
MAX_DISTANCE = 128
D_FF = -(-8 * D_MODEL // (3 * 256)) * 256
EPS = 1e-6
NEG_INF = -1e30

LANES = 128
SUBLANES = 8
VMEM_LIMIT_BYTES = 56 * 1024 * 1024

CONV_HALO = 32
CONV_ROWS = 64
CONV_STRIDE = 2
TM_CONV = 512
TM_FFN = 512
TF_FFN = 256
TQ_ATT = 512

F32 = jnp.float32
BF16 = jnp.bfloat16


def _rms(x, g):
    return x * lax.rsqrt(jnp.mean(x * x, axis=-1, keepdims=True) + EPS) * g


def _const_spec(shape):
    nd = len(shape)
    return pl.BlockSpec(shape, lambda *_: (0,) * nd, pipeline_mode=pl.Buffered(1))


def _conv_mixer_kernel(x_ref, g_ref, w1_ref, b1_ref, wdw_ref, bdw_ref, lng_ref,
                       lnb_ref, w2_ref, b2_ref, o_ref, abuf, ybuf):
    tm = x_ref.shape[1]
    x = x_ref[0]
    xn = _rms(x, g_ref[...]).astype(BF16)
    pre = jnp.dot(xn, w1_ref[...], preferred_element_type=F32) + b1_ref[...]
    a = pre[:, :D_MODEL] * jax.nn.sigmoid(pre[:, D_MODEL:])

    n_slab = D_MODEL // LANES

    @pl.when(pl.program_id(1) == 0)
    def _():
        abuf[:, 0:CONV_HALO, :] = jnp.zeros((n_slab, CONV_HALO, LANES), F32)

    for c in range(n_slab):
        abuf[c, CONV_HALO:CONV_HALO + tm, :] = a[:, c * LANES:(c + 1) * LANES]

    first_tap = CONV_HALO - (CONV_WIDTH - 1)
    half = CONV_ROWS // CONV_STRIDE

    def chunk(i, carry):
        r0 = i * CONV_ROWS
        for c in range(n_slab):
            ls = slice(c * LANES, (c + 1) * LANES)
            accs = [jnp.broadcast_to(bdw_ref[:, ls], (half, LANES))] * CONV_STRIDE
            for k in range(CONV_WIDTH):
                w = wdw_ref[k:k + 1, ls]
                for j in range(CONV_STRIDE):
                    win = abuf[c, pl.ds(r0 + j + first_tap + k, half, stride=CONV_STRIDE), :]
                    accs[j] = accs[j] + win * w
            for j in range(CONV_STRIDE):
                ybuf[c, pl.ds(r0 + j, half, stride=CONV_STRIDE), :] = accs[j]
        return carry

    lax.fori_loop(0, tm // CONV_ROWS, chunk, 0)
    abuf[:, 0:CONV_HALO, :] = abuf[:, tm:tm + CONV_HALO, :]

    y = jnp.concatenate([ybuf[c] for c in range(n_slab)], axis=-1)
    mu = jnp.mean(y, axis=-1, keepdims=True)
    yc = y - mu
    var = jnp.mean(yc * yc, axis=-1, keepdims=True)
    z = yc * lax.rsqrt(var + EPS) * lng_ref[...] + lnb_ref[...]
    z = (z * jax.nn.sigmoid(z)).astype(BF16)
    o_ref[0] = x + jnp.dot(z, w2_ref[...], preferred_element_type=F32) + b2_ref[...]


def _conv_mixer(h, g, w1, b1, wdw, bdw, lng, lnb, w2, b2):
    B, S, D = h.shape
    tm = TM_CONV
    row = lambda v: v.reshape(1, -1)
    return pl.pallas_call(
        _conv_mixer_kernel,
        grid=(B, S // tm),
        in_specs=[
            pl.BlockSpec((1, tm, D), lambda b, j: (b, j, 0)),
            _const_spec((1, D)),
            _const_spec((D, 2 * D)),
            _const_spec((1, 2 * D)),
            _const_spec((CONV_WIDTH, D)),
            _const_spec((1, D)),
            _const_spec((1, D)),
            _const_spec((1, D)),
            _const_spec((D, D)),
            _const_spec((1, D)),
        ],
        out_specs=pl.BlockSpec((1, tm, D), lambda b, j: (b, j, 0)),
        out_shape=jax.ShapeDtypeStruct((B, S, D), F32),
        scratch_shapes=[
            pltpu.VMEM((D // LANES, CONV_HALO + tm, LANES), F32),
            pltpu.VMEM((D // LANES, tm, LANES), F32),
        ],
        compiler_params=pltpu.CompilerParams(
            dimension_semantics=("arbitrary", "arbitrary"),
            vmem_limit_bytes=VMEM_LIMIT_BYTES),
        name="conv_mixer",
    )(h, row(g), w1.astype(BF16), row(b1), wdw, row(bdw), row(lng), row(lnb),
      w2.astype(BF16), row(b2))


def _ffn_kernel(*refs, emit_kv, final_norm):
    x_ref, g_ref, wg_ref, wu_ref, wd_ref = refs[:5]
    rest = refs[5:]
    x = x_ref[...]
    xn = _rms(x, g_ref[...]).astype(BF16)
    acc = x
    for c in range(D_FF // TF_FFN):
        cs = slice(c * TF_FFN, (c + 1) * TF_FFN)
        gate = jnp.dot(xn, wg_ref[:, cs], preferred_element_type=F32)
        up = jnp.dot(xn, wu_ref[:, cs], preferred_element_type=F32)
        act = (gate * jax.nn.sigmoid(gate) * up).astype(BF16)
        acc = acc + jnp.dot(act, wd_ref[cs, :], preferred_element_type=F32)
    if emit_kv:
        gkv_ref, wkv_ref, o_ref, kv_ref = rest
        o_ref[...] = acc
        hn = _rms(acc, gkv_ref[...]).astype(BF16)
        kv_ref[...] = jnp.dot(hn, wkv_ref[...], preferred_element_type=F32).astype(BF16)
    elif final_norm:
        gf_ref, o_ref = rest
        o_ref[...] = _rms(acc, gf_ref[...])
    else:
        (o_ref,) = rest
        o_ref[...] = acc


def _ffn(h2d, g, w_up, w_down, *, kv=None, final_g=None):
    M, D = h2d.shape
    tm = TM_FFN
    wg = w_up[:, :D_FF].astype(BF16)
    wu = w_up[:, D_FF:].astype(BF16)
    wd = w_down.astype(BF16)
    tile = pl.BlockSpec((tm, D), lambda i: (i, 0))
    in_specs = [tile, _const_spec((1, D)), _const_spec((D, D_FF)),
                _const_spec((D, D_FF)), _const_spec((D_FF, D))]
    args = [h2d, g.reshape(1, D), wg, wu, wd]
    out_specs = tile
    out_shape = jax.ShapeDtypeStruct((M, D), F32)
    if kv is not None:
        g_kv, w_kv2 = kv
        in_specs += [_const_spec((1, D)), _const_spec(w_kv2.shape)]
        args += [g_kv.reshape(1, D), w_kv2]
        nkv = w_kv2.shape[1]
        out_specs = [tile, pl.BlockSpec((tm, nkv), lambda i: (i, 0))]
        out_shape = [out_shape, jax.ShapeDtypeStruct((M, nkv), BF16)]
    elif final_g is not None:
        in_specs += [_const_spec((1, D))]
        args += [final_g.reshape(1, D)]
    return pl.pallas_call(
        functools.partial(_ffn_kernel, emit_kv=kv is not None,
                          final_norm=final_g is not None),
        grid=(M // tm,),
        in_specs=in_specs,
        out_specs=out_specs,
        out_shape=out_shape,
        compiler_params=pltpu.CompilerParams(
            dimension_semantics=("arbitrary",),
            vmem_limit_bytes=VMEM_LIMIT_BYTES),
        name="ffn",
    )(*args)


def _band_bucket_table():
    qi = jnp.arange(BLOCK, dtype=jnp.int32)
    kj = jnp.arange(2 * BLOCK, dtype=jnp.int32)
    dist = qi[:, None] + BLOCK - kj[None, :]
    in_window = (dist >= 0) & (dist < WINDOW)
    max_exact = N_BUCKETS // 2
    d = jnp.maximum(dist, 0)
    log_ratio = jnp.log(jnp.maximum(d, 1).astype(F32) / max_exact) / math.log(MAX_DISTANCE / max_exact)
    large = max_exact + (log_ratio * (N_BUCKETS - max_exact)).astype(jnp.int32)
    large = jnp.minimum(large, N_BUCKETS - 1)
    bucket = jnp.where(d < max_exact, d, large)
    inner = jnp.where(in_window, bucket, -1)
    first = jnp.where(in_window & (kj[None, :] >= BLOCK), bucket, -1)
    return jnp.stack([inner, first])


def _bias_table_kernel(rb_ref, bucket_ref, o_ref):
    h = pl.program_id(0)
    bucket = bucket_ref[...]
    acc = jnp.full(bucket.shape, NEG_INF, F32)
    for j in range(N_BUCKETS):
        acc = jnp.where(bucket == j, rb_ref[j, h], acc)
    o_ref[:, 0] = acc


def _bias_table(rel_bias):
    bucket = _band_bucket_table()
    return pl.pallas_call(
        _bias_table_kernel,
        grid=(N_HEADS,),
        in_specs=[pl.BlockSpec(memory_space=pltpu.SMEM),
                  pl.BlockSpec(bucket.shape, lambda h: (0, 0, 0))],
        out_specs=pl.BlockSpec((2, 1, BLOCK, 2 * BLOCK), lambda h: (0, h, 0, 0)),
        out_shape=jax.ShapeDtypeStruct((2, N_HEADS, BLOCK, 2 * BLOCK), F32),
        name="bias_table",
    )(rel_bias, bucket)


def _attn_mixer_kernel(sink_ref, x_ref, g_ref, wq_ref, kvc_ref, kvp_ref, bias_ref,
                       wo_ref, o_ref, q_buf, attn_buf):
    tq = x_ref.shape[1]
    kw = N_KV_HEADS * LANES
    x = x_ref[0]
    xn = _rms(x, g_ref[...]).astype(BF16)
    q = jnp.dot(xn, wq_ref[...], preferred_element_type=F32) * (HEAD_DIM ** -0.5)
    q_buf[...] = q.astype(BF16)
    low = lax.broadcasted_iota(jnp.int32, (BLOCK, LANES), 1) < HEAD_DIM
    zero = jnp.zeros((BLOCK, LANES), BF16)

    for qb in range(tq // BLOCK):
        rows = slice(qb * BLOCK, (qb + 1) * BLOCK)
        if qb == 0:
            sel = jnp.where(pl.program_id(1) == 0, 1, 0)
        else:
            sel = 0
        for kvh in range(N_KV_HEADS):
            kl = slice(kvh * LANES, (kvh + 1) * LANES)
            vl = slice(kw + kvh * LANES, kw + (kvh + 1) * LANES)
            if qb == 0:
                k_band = jnp.concatenate([kvp_ref[0, :, kl], kvc_ref[0, 0:BLOCK, kl]], axis=0)
                v_band = jnp.concatenate([kvp_ref[0, :, vl], kvc_ref[0, 0:BLOCK, vl]], axis=0)
            else:
                band = slice((qb - 1) * BLOCK, (qb + 1) * BLOCK)
                k_band = kvc_ref[0, band, kl]
                v_band = kvc_ref[0, band, vl]
            parts = []
            for p in range(GROUP // 2):
                q2 = q_buf[rows, (kvh * 2 + p) * LANES:(kvh * 2 + p + 1) * LANES]
                parts += [jnp.where(low, q2, zero), jnp.where(low, zero, q2)]
            qs = jnp.concatenate(parts, axis=0)
            s = lax.dot_general(qs, k_band, (((1,), (1,)), ((), ())),
                                preferred_element_type=F32)
            outs = []
            for i in range(GROUP):
                head = kvh * GROUP + i
                sink = sink_ref[head]
                si = s[i * BLOCK:(i + 1) * BLOCK] + bias_ref[sel, head]
                m = jnp.maximum(jnp.max(si, axis=-1, keepdims=True), sink)
                pexp = jnp.exp(si - m)
                denom = jnp.sum(pexp, axis=-1, keepdims=True) + jnp.exp(sink - m)
                oi = jnp.dot(pexp.astype(BF16), v_band, preferred_element_type=F32)
                outs.append(oi / denom)
            for p in range(GROUP // 2):
                merged = jnp.where(low, outs[2 * p], outs[2 * p + 1])
                attn_buf[rows, (kvh * 2 + p) * LANES:(kvh * 2 + p + 1) * LANES] = merged.astype(BF16)

    o_ref[0] = x + jnp.dot(attn_buf[...], wo_ref[...], preferred_element_type=F32)


def _attn_mixer(h, g, wq, kv, bias, sinks, wo):
    B, S, D = h.shape
    tq = TQ_ATT
    nkv = kv.shape[-1]
    per = tq // BLOCK
    return pl.pallas_call(
        _attn_mixer_kernel,
        grid=(B, S // tq),
        in_specs=[
            pl.BlockSpec(memory_space=pltpu.SMEM),
            pl.BlockSpec((1, tq, D), lambda b, j: (b, j, 0)),
            _const_spec((1, D)),
            _const_spec((D, D)),
            pl.BlockSpec((1, tq, nkv), lambda b, j: (b, j, 0)),
            pl.BlockSpec((1, BLOCK, nkv), lambda b, j: (b, jnp.maximum(j * per - 1, 0), 0)),
            _const_spec(bias.shape),
            _const_spec((D, D)),
        ],
        out_specs=pl.BlockSpec((1, tq, D), lambda b, j: (b, j, 0)),
        out_shape=jax.ShapeDtypeStruct((B, S, D), F32),
        scratch_shapes=[pltpu.VMEM((tq, D), BF16), pltpu.VMEM((tq, D), BF16)],
        compiler_params=pltpu.CompilerParams(
            dimension_semantics=("arbitrary", "arbitrary"),
            vmem_limit_bytes=VMEM_LIMIT_BYTES),
        name="attn_mixer",
    )(sinks, h, g.reshape(1, D), wq.astype(BF16), kv, kv, bias, wo.astype(BF16))


def _dup_heads(w):
    d = w.shape[0]
    w = w.reshape(d, N_KV_HEADS, 1, HEAD_DIM)
    return jnp.broadcast_to(w, (d, N_KV_HEADS, LANES // HEAD_DIM, HEAD_DIM)).reshape(d, N_KV_HEADS * LANES)


def kernel(x, norm_mix, norm_ffn, conv_w_pw1, conv_b_pw1, conv_w_dw, conv_b_dw, conv_ln_g, conv_ln_b, conv_w_pw2, conv_b_pw2, norm_kv, w_kv, w_q, w_o, sinks, rel_bias, ffn_w_up, ffn_w_down, norm_final):
    B, S, D = x.shape
    kvd = N_KV_HEADS * HEAD_DIM
    w_kv2 = jnp.concatenate([_dup_heads(w_kv[:, :kvd]), _dup_heads(w_kv[:, kvd:])], axis=1).astype(BF16)
    bias = _bias_table(rel_bias)
    h = x
    kv = None
    for l in range(DEPTH):
        if l < N_A_LAYERS:
            h = _conv_mixer(h, norm_mix[l], conv_w_pw1[l], conv_b_pw1[l], conv_w_dw[l],
                            conv_b_dw[l], conv_ln_g[l], conv_ln_b[l], conv_w_pw2[l], conv_b_pw2[l])
        else:
            j = l - N_A_LAYERS
            h = _attn_mixer(h, norm_mix[l], w_q[j], kv, bias, sinks[j], w_o[j])
        h2d = h.reshape(B * S, D)
        if l == N_A_LAYERS - 1:
            h2d, kv2d = _ffn(h2d, norm_ffn[l], ffn_w_up[l], ffn_w_down[l], kv=(norm_kv, w_kv2))
            kv = kv2d.reshape(B, S, -1)
        elif l == DEPTH - 1:
            h2d = _ffn(h2d, norm_ffn[l], ffn_w_up[l], ffn_w_down[l], final_g=norm_final)
        else:
            h2d = _ffn(h2d, norm_ffn[l], ffn_w_up[l], ffn_w_down[l])
        h = h2d.reshape(B, S, D)
    return h
```

```python
import functools
import math

import jax
import jax.numpy as jnp
from jax import lax
from jax.experimental import pallas as pl
from jax.experimental.pallas import tpu as pltpu

D_MODEL = 1024
DEPTH = 4
N_A_LAYERS = DEPTH // 2
CONV_WIDTH = 31
HEAD_DIM = 64
N_HEADS = D_MODEL // HEAD_DIM
N_KV_HEADS = N_HEADS // 4
GROUP = N_HEADS // N_KV_HEADS
WINDOW = 128
BLOCK = 128
N_BUCKETS = 32
MAX_DISTANCE = 128
D_FF = -(-8 * D_MODEL // (3 * 256)) * 256
EPS = 1e-6
NEG_INF = -1e30

LANES = 128
VMEM_LIMIT_BYTES = 56 * 1024 * 1024

CONV_HALO = 32
CONV_ROWS = 48
CONV_STRIDE = 2
TM_CONV = 512
TM_FFN = 512
TF_FFN = 256
TQ_ATT = 512

F32 = jnp.float32
BF16 = jnp.bfloat16


def _rms(x, g):
    return x * lax.rsqrt(jnp.mean(x * x, axis=-1, keepdims=True) + EPS) * g


def _const_spec(shape):
    nd = len(shape)
    return pl.BlockSpec(shape, lambda *_: (0,) * nd, pipeline_mode=pl.Buffered(1))


def _conv_ffn_kernel(*refs, emit_kv, tiles_per_seq):
    (x_ref, gm_ref, w1_ref, b1_ref, wdw_ref, bdw_ref, lng_ref, lnb_ref, w2_ref,
     b2_ref, gf_ref, wg_ref, wu_ref, wd_ref) = refs[:14]
    if emit_kv:
        gkv_ref, wkv_ref, o_ref, kv_ref, abuf, ybuf, hmid, xn_s, acc, act_s = refs[14:]
    else:
        o_ref, abuf, ybuf, hmid, xn_s, acc, act_s = refs[14:]
    i = pl.program_id(0)
    tm = x_ref.shape[0]
    n_chunk = wg_ref.shape[0]
    n_slab = D_MODEL // LANES
    conv_rows = abuf.shape[1] - CONV_HALO

    @pl.when(i == 0)
    def _():
        hmid[...] = jnp.zeros(hmid.shape, F32)
        xn_s[...] = jnp.zeros(xn_s.shape, BF16)
        abuf[:, CONV_HALO + tm:, :] = jnp.zeros((n_slab, conv_rows - tm, LANES), F32)

    @pl.when(lax.rem(i, tiles_per_seq) == 0)
    def _():
        abuf[:, 0:CONV_HALO, :] = jnp.zeros((n_slab, CONV_HALO, LANES), F32)

    xn = _rms(x_ref[...], gm_ref[...]).astype(BF16)
    pre = jnp.dot(xn, w1_ref[...], preferred_element_type=F32) + b1_ref[...]
    a = pre[:, :D_MODEL] * jax.nn.sigmoid(pre[:, D_MODEL:])
    for c in range(n_slab):
        abuf[c, CONV_HALO:CONV_HALO + tm, :] = a[:, c * LANES:(c + 1) * LANES]

    first_tap = CONV_HALO - (CONV_WIDTH - 1)
    half = CONV_ROWS // CONV_STRIDE
    acc[...] = hmid[...]

    def up_proj(c, slot):
        xp = xn_s[...]
        gate = jnp.dot(xp, wg_ref[c], preferred_element_type=F32)
        up = jnp.dot(xp, wu_ref[c], preferred_element_type=F32)
        act_s[slot] = (gate * jax.nn.sigmoid(gate) * up).astype(BF16)

    def down_proj(c, slot):
        acc[...] += jnp.dot(act_s[slot], wd_ref[c], preferred_element_type=F32)

    def conv_rows_chunk(c):
        r0 = c * CONV_ROWS
        for s in range(n_slab):
            ls = slice(s * LANES, (s + 1) * LANES)
            accs = [jnp.broadcast_to(bdw_ref[:, ls], (half, LANES))] * CONV_STRIDE
            for k in range(CONV_WIDTH):
                w = wdw_ref[k:k + 1, ls]
                for j in range(CONV_STRIDE):
                    win = abuf[s, pl.ds(r0 + j + first_tap + k, half, stride=CONV_STRIDE), :]
                    accs[j] = accs[j] + win * w
            for j in range(CONV_STRIDE):
                ybuf[s, pl.ds(r0 + j, half, stride=CONV_STRIDE), :] = accs[j]

    up_proj(0, 0)
    down_proj(0, 0)
    conv_rows_chunk(0)

    def trip(t, carry):
        c = 2 * t + 1
        up_proj(c, 0)
        up_proj(c + 1, 1)
        down_proj(c, 0)
        down_proj(c + 1, 1)
        conv_rows_chunk(c)
        conv_rows_chunk(c + 1)
        return carry

    assert n_chunk % 2 == 1
    lax.fori_loop(0, n_chunk // 2, trip, 0)

    out = acc[...]
    o_ref[...] = out
    if emit_kv:
        hn = _rms(out, gkv_ref[...]).astype(BF16)
        kv_ref[...] = jnp.dot(hn, wkv_ref[...], preferred_element_type=F32).astype(BF16)

    abuf[:, 0:CONV_HALO, :] = abuf[:, tm:tm + CONV_HALO, :]
    y = jnp.concatenate([ybuf[s, 0:tm, :] for s in range(n_slab)], axis=-1)
    mu = jnp.mean(y, axis=-1, keepdims=True)
    yc = y - mu
    var = jnp.mean(yc * yc, axis=-1, keepdims=True)
    z = yc * lax.rsqrt(var + EPS) * lng_ref[...] + lnb_ref[...]
    z = (z * jax.nn.sigmoid(z)).astype(BF16)
    m = x_ref[...] + jnp.dot(z, w2_ref[...], preferred_element_type=F32) + b2_ref[...]
    hmid[...] = m
    xn_s[...] = _rms(m, gf_ref[...]).astype(BF16)


def _conv_ffn(h2d, seq_len, g_mix, w1, b1, wdw, bdw, lng, lnb, w2, b2, g_ffn, w_up,
              w_down, *, kv=None):
    M, D = h2d.shape
    tm = TM_CONV
    nt = M // tm
    n_chunk = D_FF // TF_FFN
    assert n_chunk * CONV_ROWS >= tm and seq_len % tm == 0
    row = lambda v: v.reshape(1, -1)
    chunked = lambda w: w.reshape(D, n_chunk, TF_FFN).transpose(1, 0, 2).astype(BF16)
    wg = chunked(w_up[:, :D_FF])
    wu = chunked(w_up[:, D_FF:])
    wd = w_down.reshape(n_chunk, TF_FFN, D).astype(BF16)
    in_tile = pl.BlockSpec((tm, D), lambda i: (jnp.minimum(i, nt - 1), 0))
    out_tile = lambda n: pl.BlockSpec((tm, n), lambda i: (jnp.maximum(i - 1, 0), 0))
    in_specs = [in_tile, _const_spec((1, D)), _const_spec((D, 2 * D)), _const_spec((1, 2 * D)),
                _const_spec((CONV_WIDTH, D)), _const_spec((1, D)), _const_spec((1, D)),
                _const_spec((1, D)), _const_spec((D, D)), _const_spec((1, D)),
                _const_spec((1, D)), _const_spec(wg.shape), _const_spec(wu.shape),
                _const_spec(wd.shape)]
    args = [h2d, row(g_mix), w1.astype(BF16), row(b1), wdw, row(bdw), row(lng), row(lnb),
            w2.astype(BF16), row(b2), row(g_ffn), wg, wu, wd]
    out_specs = out_tile(D)
    out_shape = jax.ShapeDtypeStruct((M, D), F32)
    if kv is not None:
        g_kv, w_kv2 = kv
        nkv = w_kv2.shape[1]
        in_specs += [_const_spec((1, D)), _const_spec(w_kv2.shape)]
        args += [row(g_kv), w_kv2]
        out_specs = [out_specs, out_tile(nkv)]
        out_shape = [out_shape, jax.ShapeDtypeStruct((M, nkv), BF16)]
    conv_rows = n_chunk * CONV_ROWS
    return pl.pallas_call(
        functools.partial(_conv_ffn_kernel, emit_kv=kv is not None,
                          tiles_per_seq=seq_len // tm),
        grid=(nt + 1,),
        in_specs=in_specs,
        out_specs=out_specs,
        out_shape=out_shape,
        scratch_shapes=[
            pltpu.VMEM((D // LANES, CONV_HALO + conv_rows, LANES), F32),
            pltpu.VMEM((D // LANES, conv_rows, LANES), F32),
            pltpu.VMEM((tm, D), F32),
            pltpu.VMEM((tm, D), BF16),
            pltpu.VMEM((tm, D), F32),
            pltpu.VMEM((2, tm, TF_FFN), BF16),
        ],
        compiler_params=pltpu.CompilerParams(
            dimension_semantics=("arbitrary",),
            vmem_limit_bytes=VMEM_LIMIT_BYTES),
        name="conv_ffn",
    )(*args)


def _ffn_kernel(*refs, final_norm):
    x_ref, g_ref, wg_ref, wu_ref, wd_ref = refs[:5]
    x = x_ref[...]
    xn = _rms(x, g_ref[...]).astype(BF16)
    acc = x
    for c in range(D_FF // TF_FFN):
        cs = slice(c * TF_FFN, (c + 1) * TF_FFN)
        gate = jnp.dot(xn, wg_ref[:, cs], preferred_element_type=F32)
        up = jnp.dot(xn, wu_ref[:, cs], preferred_element_type=F32)
        act = (gate * jax.nn.sigmoid(gate) * up).astype(BF16)
        acc = acc + jnp.dot(act, wd_ref[cs, :], preferred_element_type=F32)
    if final_norm:
        gf_ref, o_ref = refs[5:]
        o_ref[...] = _rms(acc, gf_ref[...])
    else:
        (o_ref,) = refs[5:]
        o_ref[...] = acc


def _ffn(h2d, g, w_up, w_down, *, final_g=None):
    M, D = h2d.shape
    tm = TM_FFN
    wg = w_up[:, :D_FF].astype(BF16)
    wu = w_up[:, D_FF:].astype(BF16)
    wd = w_down.astype(BF16)
    tile = pl.BlockSpec((tm, D), lambda i: (i, 0))
    in_specs = [tile, _const_spec((1, D)), _const_spec((D, D_FF)),
                _const_spec((D, D_FF)), _const_spec((D_FF, D))]
    args = [h2d, g.reshape(1, D), wg, wu, wd]
    if final_g is not None:
        in_specs += [_const_spec((1, D))]
        args += [final_g.reshape(1, D)]
    return pl.pallas_call(
        functools.partial(_ffn_kernel, final_norm=final_g is not None),
        grid=(M // tm,),
        in_specs=in_specs,
        out_specs=tile,
        out_shape=jax.ShapeDtypeStruct((M, D), F32),
        compiler_params=pltpu.CompilerParams(
            dimension_semantics=("arbitrary",),
            vmem_limit_bytes=VMEM_LIMIT_BYTES),
        name="ffn",
    )(*args)


def _band_bucket_table():
    qi = jnp.arange(BLOCK, dtype=jnp.int32)
    kj = jnp.arange(2 * BLOCK, dtype=jnp.int32)
    dist = qi[:, None] + BLOCK - kj[None, :]
    in_window = (dist >= 0) & (dist < WINDOW)
    max_exact = N_BUCKETS // 2
    d = jnp.maximum(dist, 0)
    log_ratio = jnp.log(jnp.maximum(d, 1).astype(F32) / max_exact) / math.log(MAX_DISTANCE / max_exact)
    large = max_exact + (log_ratio * (N_BUCKETS - max_exact)).astype(jnp.int32)
    large = jnp.minimum(large, N_BUCKETS - 1)
    bucket = jnp.where(d < max_exact, d, large)
    inner = jnp.where(in_window, bucket, -1)
    first = jnp.where(in_window & (kj[None, :] >= BLOCK), bucket, -1)
    return jnp.stack([inner, first])


def _bias_table_kernel(rb_ref, bucket_ref, o_ref):
    h = pl.program_id(0)
    bucket = bucket_ref[...]
    acc = jnp.full(bucket.shape, NEG_INF, F32)
    for j in range(N_BUCKETS):
        acc = jnp.where(bucket == j, rb_ref[j, h], acc)
    o_ref[:, 0] = acc


def _bias_table(rel_bias):
    bucket = _band_bucket_table()
    return pl.pallas_call(
        _bias_table_kernel,
        grid=(N_HEADS,),
        in_specs=[pl.BlockSpec(memory_space=pltpu.SMEM),
                  pl.BlockSpec(bucket.shape, lambda h: (0, 0, 0))],
        out_specs=pl.BlockSpec((2, 1, BLOCK, 2 * BLOCK), lambda h: (0, h, 0, 0)),
        out_shape=jax.ShapeDtypeStruct((2, N_HEADS, BLOCK, 2 * BLOCK), F32),
        name="bias_table",
    )(rel_bias, bucket)


def _attn_mixer_kernel(sink_ref, x_ref, g_ref, wq_ref, kvc_ref, kvp_ref, bias_ref,
                       wo_ref, o_ref, q_buf, attn_buf):
    tq = x_ref.shape[1]
    kw = N_KV_HEADS * LANES
    x = x_ref[0]
    xn = _rms(x, g_ref[...]).astype(BF16)
    q = jnp.dot(xn, wq_ref[...], preferred_element_type=F32) * (HEAD_DIM ** -0.5)
    q_buf[...] = q.astype(BF16)
    low = lax.broadcasted_iota(jnp.int32, (BLOCK, LANES), 1) < HEAD_DIM
    zero = jnp.zeros((BLOCK, LANES), BF16)

    for qb in range(tq // BLOCK):
        rows = slice(qb * BLOCK, (qb + 1) * BLOCK)
        if qb == 0:
            sel = jnp.where(pl.program_id(1) == 0, 1, 0)
        else:
            sel = 0
        for kvh in range(N_KV_HEADS):
            kl = slice(kvh * LANES, (kvh + 1) * LANES)
            vl = slice(kw + kvh * LANES, kw + (kvh + 1) * LANES)
            if qb == 0:
                k_band = jnp.concatenate([kvp_ref[0, :, kl], kvc_ref[0, 0:BLOCK, kl]], axis=0)
                v_band = jnp.concatenate([kvp_ref[0, :, vl], kvc_ref[0, 0:BLOCK, vl]], axis=0)
            else:
                band = slice((qb - 1) * BLOCK, (qb + 1) * BLOCK)
                k_band = kvc_ref[0, band, kl]
                v_band = kvc_ref[0, band, vl]
            parts = []
            for p in range(GROUP // 2):
                q2 = q_buf[rows, (kvh * 2 + p) * LANES:(kvh * 2 + p + 1) * LANES]
                parts += [jnp.where(low, q2, zero), jnp.where(low, zero, q2)]
            qs = jnp.concatenate(parts, axis=0)
            s = lax.dot_general(qs, k_band, (((1,), (1,)), ((), ())),
                                preferred_element_type=F32)
            outs = []
            for i in range(GROUP):
                head = kvh * GROUP + i
                sink = sink_ref[head]
                si = s[i * BLOCK:(i + 1) * BLOCK] + bias_ref[sel, head]
                m = jnp.maximum(jnp.max(si, axis=-1, keepdims=True), sink)
                pexp = jnp.exp(si - m)
                denom = jnp.sum(pexp, axis=-1, keepdims=True) + jnp.exp(sink - m)
                oi = jnp.dot(pexp.astype(BF16), v_band, preferred_element_type=F32)
                outs.append(oi / denom)
            for p in range(GROUP // 2):
                merged = jnp.where(low, outs[2 * p], outs[2 * p + 1])
                attn_buf[rows, (kvh * 2 + p) * LANES:(kvh * 2 + p + 1) * LANES] = merged.astype(BF16)

    o_ref[0] = x + jnp.dot(attn_buf[...], wo_ref[...], preferred_element_type=F32)


def _attn_mixer(h, g, wq, kv, bias, sinks, wo):
    B, S, D = h.shape
    tq = TQ_ATT
    nkv = kv.shape[-1]
    per = tq // BLOCK
    return pl.pallas_call(
        _attn_mixer_kernel,
        grid=(B, S // tq),
        in_specs=[
            pl.BlockSpec(memory_space=pltpu.SMEM),
            pl.BlockSpec((1, tq, D), lambda b, j: (b, j, 0)),
            _const_spec((1, D)),
            _const_spec((D, D)),
            pl.BlockSpec((1, tq, nkv), lambda b, j: (b, j, 0)),
            pl.BlockSpec((1, BLOCK, nkv), lambda b, j: (b, jnp.maximum(j * per - 1, 0), 0)),
            _const_spec(bias.shape),
            _const_spec((D, D)),
        ],
        out_specs=pl.BlockSpec((1, tq, D), lambda b, j: (b, j, 0)),
        out_shape=jax.ShapeDtypeStruct((B, S, D), F32),
        scratch_shapes=[pltpu.VMEM((tq, D), BF16), pltpu.VMEM((tq, D), BF16)],
        compiler_params=pltpu.CompilerParams(
            dimension_semantics=("arbitrary", "arbitrary"),
            vmem_limit_bytes=VMEM_LIMIT_BYTES),
        name="attn_mixer",
    )(sinks, h, g.reshape(1, D), wq.astype(BF16), kv, kv, bias, wo.astype(BF16))


def _dup_heads(w):
    d = w.shape[0]
    w = w.reshape(d, N_KV_HEADS, 1, HEAD_DIM)
    return jnp.broadcast_to(w, (d, N_KV_HEADS, LANES // HEAD_DIM, HEAD_DIM)).reshape(d, N_KV_HEADS * LANES)


def kernel(x, norm_mix, norm_ffn, conv_w_pw1, conv_b_pw1, conv_w_dw, conv_b_dw, conv_ln_g, conv_ln_b, conv_w_pw2, conv_b_pw2, norm_kv, w_kv, w_q, w_o, sinks, rel_bias, ffn_w_up, ffn_w_down, norm_final):
    B, S, D = x.shape
    kvd = N_KV_HEADS * HEAD_DIM
    w_kv2 = jnp.concatenate([_dup_heads(w_kv[:, :kvd]), _dup_heads(w_kv[:, kvd:])], axis=1).astype(BF16)
    bias = _bias_table(rel_bias)
    h2d = x.reshape(B * S, D)
    kv = None
    for l in range(DEPTH):
        if l < N_A_LAYERS:
            args = (h2d, S, norm_mix[l], conv_w_pw1[l], conv_b_pw1[l], conv_w_dw[l], conv_b_dw[l],
                    conv_ln_g[l], conv_ln_b[l], conv_w_pw2[l], conv_b_pw2[l], norm_ffn[l],
                    ffn_w_up[l], ffn_w_down[l])
            if l == N_A_LAYERS - 1:
                h2d, kv2d = _conv_ffn(*args, kv=(norm_kv, w_kv2))
                kv = kv2d.reshape(B, S, -1)
            else:
                h2d = _conv_ffn(*args)
        else:
            j = l - N_A_LAYERS
            h = _attn_mixer(h2d.reshape(B, S, D), norm_mix[l], w_q[j], kv, bias, sinks[j], w_o[j])
            h2d = h.reshape(B * S, D)
            if l == DEPTH - 1:
                h2d = _ffn(h2d, norm_ffn[l], ffn_w_up[l], ffn_w_down[l], final_g=norm_final)
            else:
                h2d = _ffn(h2d, norm_ffn[l], ffn_w_up[l], ffn_w_down[l])
    return h2d.reshape(B, S, D)
```

```python
import functools
import math

import jax
import jax.numpy as jnp
from jax import lax
from jax.experimental import pallas as pl
from jax.experimental.pallas import tpu as pltpu

D_MODEL = 1024
DEPTH = 4
N_A_LAYERS = DEPTH // 2
CONV_WIDTH = 31
HEAD_DIM = 64
N_HEADS = D_MODEL // HEAD_DIM
N_KV_HEADS = N_HEADS // 4
GROUP = N_HEADS // N_KV_HEADS
WINDOW = 128
BLOCK = 128
N_BUCKETS = 32
MAX_DISTANCE = 128
D_FF = -(-8 * D_MODEL // (3 * 256)) * 256
EPS = 1e-6
NEG_INF = -1e30

LANES = 128
VMEM_LIMIT_BYTES = 56 * 1024 * 1024

CONV_HALO = 32
CONV_ROWS = 64
CONV_STRIDE = 2
TM_CONV = 512
TM_FFN = 512
TF_FFN = 256
TQ_ATT = 512

F32 = jnp.float32
BF16 = jnp.bfloat16


def _rms(x, g):
    return x * lax.rsqrt(jnp.mean(x * x, axis=-1, keepdims=True) + EPS) * g


def _const_spec(shape):
    nd = len(shape)
    return pl.BlockSpec(shape, lambda *_: (0,) * nd, pipeline_mode=pl.Buffered(1))


def _conv_mixer_kernel(x_ref, g_ref, w1_ref, b1_ref, wdw_ref, bdw_ref, lng_ref,
                       lnb_ref, w2_ref, b2_ref, o_ref, abuf_even, abuf_odd, ybuf, xn_s):
    tm = x_ref.shape[1]
    n_slab = D_MODEL // LANES
    n_pair = n_slab // 2
    xn_s[...] = _rms(x_ref[0], g_ref[...]).astype(BF16)

    abufs = (abuf_even, abuf_odd)

    @pl.when(pl.program_id(1) == 0)
    def _():
        for buf in abufs:
            buf[:, 0:CONV_HALO, :] = jnp.zeros((n_pair, CONV_HALO, LANES), F32)

    def glu_slab(parity, idx):
        c = 2 * idx + parity
        pre = jnp.dot(xn_s[...], w1_ref[c], preferred_element_type=F32) + b1_ref[c]
        abufs[parity][idx, CONV_HALO:CONV_HALO + tm, :] = (
            pre[:, :LANES] * jax.nn.sigmoid(pre[:, LANES:]))

    first_tap = CONV_HALO - (CONV_WIDTH - 1)
    half = CONV_ROWS // CONV_STRIDE

    def conv_slab(parity, idx):
        c = 2 * idx + parity
        src = abufs[parity]
        for r0 in range(0, tm, CONV_ROWS):
            accs = [jnp.broadcast_to(bdw_ref[c], (half, LANES))] * CONV_STRIDE
            for k in range(CONV_WIDTH):
                w = wdw_ref[c, k:k + 1, :]
                for j in range(CONV_STRIDE):
                    win = src[idx, pl.ds(r0 + j + first_tap + k, half, stride=CONV_STRIDE), :]
                    accs[j] = accs[j] + win * w
            for j in range(CONV_STRIDE):
                ybuf[c, pl.ds(r0 + j, half, stride=CONV_STRIDE), :] = accs[j]

    glu_slab(0, 0)

    for t in range(n_pair):
        glu_slab(1, t)
        conv_slab(0, t)
        if t + 1 < n_pair:
            glu_slab(0, t + 1)
        conv_slab(1, t)
    for buf in abufs:
        buf[:, 0:CONV_HALO, :] = buf[:, tm:tm + CONV_HALO, :]

    y = jnp.concatenate([ybuf[c] for c in range(n_slab)], axis=-1)
    mu = jnp.mean(y, axis=-1, keepdims=True)
    yc = y - mu
    var = jnp.mean(yc * yc, axis=-1, keepdims=True)
    z = yc * lax.rsqrt(var + EPS) * lng_ref[...] + lnb_ref[...]
    z = (z * jax.nn.sigmoid(z)).astype(BF16)
    o_ref[0] = x_ref[0] + jnp.dot(z, w2_ref[...], preferred_element_type=F32) + b2_ref[...]


def _conv_mixer(h, g, w1, b1, wdw, bdw, lng, lnb, w2, b2):
    B, S, D = h.shape
    tm = TM_CONV
    n_slab = D // LANES
    row = lambda v: v.reshape(1, -1)
    slabs = lambda w: w.reshape(w.shape[0], 2, n_slab, LANES).transpose(2, 0, 1, 3).reshape(
        n_slab, w.shape[0], 2 * LANES)
    w1s = slabs(w1).astype(BF16)
    b1s = slabs(b1.reshape(1, 2 * D))
    wdws = wdw.reshape(CONV_WIDTH, n_slab, LANES).transpose(1, 0, 2)
    bdws = bdw.reshape(n_slab, 1, LANES)
    return pl.pallas_call(
        _conv_mixer_kernel,
        grid=(B, S // tm),
        in_specs=[
            pl.BlockSpec((1, tm, D), lambda b, j: (b, j, 0)),
            _const_spec((1, D)),
            _const_spec(w1s.shape),
            _const_spec(b1s.shape),
            _const_spec(wdws.shape),
            _const_spec(bdws.shape),
            _const_spec((1, D)),
            _const_spec((1, D)),
            _const_spec((D, D)),
            _const_spec((1, D)),
        ],
        out_specs=pl.BlockSpec((1, tm, D), lambda b, j: (b, j, 0)),
        out_shape=jax.ShapeDtypeStruct((B, S, D), F32),
        scratch_shapes=[
            pltpu.VMEM((n_slab // 2, CONV_HALO + tm, LANES), F32),
            pltpu.VMEM((n_slab // 2, CONV_HALO + tm, LANES), F32),
            pltpu.VMEM((n_slab, tm, LANES), F32),
            pltpu.VMEM((tm, D), BF16),
        ],
        compiler_params=pltpu.CompilerParams(
            dimension_semantics=("arbitrary", "arbitrary"),
            vmem_limit_bytes=VMEM_LIMIT_BYTES),
        name="conv_mixer",
    )(h, row(g), w1s, b1s, wdws, bdws, row(lng), row(lnb), w2.astype(BF16), row(b2))


def _ffn_kernel(*refs, emit_kv, final_norm):
    x_ref, g_ref, wup_ref, wd_ref = refs[:4]
    rest = refs[4:]
    x = x_ref[...]
    xn = _rms(x, g_ref[...]).astype(BF16)
    acc = x
    for c in range(D_FF // TF_FFN):
        lo = c * TF_FFN
        gate = jnp.dot(xn, wup_ref[:, lo:lo + TF_FFN], preferred_element_type=F32)
        up = jnp.dot(xn, wup_ref[:, D_FF + lo:D_FF + lo + TF_FFN], preferred_element_type=F32)
        act = (gate * jax.nn.sigmoid(gate) * up).astype(BF16)
        acc = acc + jnp.dot(act, wd_ref[lo:lo + TF_FFN, :], preferred_element_type=F32)
    if emit_kv:
        gkv_ref, wkv_ref, o_ref, kv_ref = rest
        o_ref[...] = acc
        hn = _rms(acc, gkv_ref[...]).astype(BF16)
        kv_ref[...] = jnp.dot(hn, wkv_ref[...], preferred_element_type=F32).astype(BF16)
    elif final_norm:
        gf_ref, o_ref = rest
        o_ref[...] = _rms(acc, gf_ref[...])
    else:
        (o_ref,) = rest
        o_ref[...] = acc


def _ffn(h2d, g, w_up, w_down, *, kv=None, final_g=None):
    M, D = h2d.shape
    tm = TM_FFN
    tile = pl.BlockSpec((tm, D), lambda i: (i, 0))
    in_specs = [tile, _const_spec((1, D)), _const_spec((D, 2 * D_FF)), _const_spec((D_FF, D))]
    args = [h2d, g.reshape(1, D), w_up.astype(BF16), w_down.astype(BF16)]
    out_specs = tile
    out_shape = jax.ShapeDtypeStruct((M, D), F32)
    if kv is not None:
        g_kv, w_kv2 = kv
        in_specs += [_const_spec((1, D)), _const_spec(w_kv2.shape)]
        args += [g_kv.reshape(1, D), w_kv2]
        nkv = w_kv2.shape[1]
        out_specs = [tile, pl.BlockSpec((tm, nkv), lambda i: (i, 0))]
        out_shape = [out_shape, jax.ShapeDtypeStruct((M, nkv), BF16)]
    elif final_g is not None:
        in_specs += [_const_spec((1, D))]
        args += [final_g.reshape(1, D)]
    return pl.pallas_call(
        functools.partial(_ffn_kernel, emit_kv=kv is not None,
                          final_norm=final_g is not None),
        grid=(M // tm,),
        in_specs=in_specs,
        out_specs=out_specs,
        out_shape=out_shape,
        compiler_params=pltpu.CompilerParams(
            dimension_semantics=("arbitrary",),
            vmem_limit_bytes=VMEM_LIMIT_BYTES),
        name="ffn",
    )(*args)


def _band_bucket_table():
    qi = jnp.arange(BLOCK, dtype=jnp.int32)
    kj = jnp.arange(2 * BLOCK, dtype=jnp.int32)
    dist = qi[:, None] + BLOCK - kj[None, :]
    in_window = (dist >= 0) & (dist < WINDOW)
    max_exact = N_BUCKETS // 2
    d = jnp.maximum(dist, 0)
    log_ratio = jnp.log(jnp.maximum(d, 1).astype(F32) / max_exact) / math.log(MAX_DISTANCE / max_exact)
    large = max_exact + (log_ratio * (N_BUCKETS - max_exact)).astype(jnp.int32)
    large = jnp.minimum(large, N_BUCKETS - 1)
    bucket = jnp.where(d < max_exact, d, large)
    inner = jnp.where(in_window, bucket, -1)
    first = jnp.where(in_window & (kj[None, :] >= BLOCK), bucket, -1)
    return jnp.stack([inner, first])


def _bias_table_kernel(rb_ref, bucket_ref, o_ref):
    h = pl.program_id(0)
    bucket = bucket_ref[...]
    acc = jnp.full(bucket.shape, NEG_INF, F32)
    for j in range(N_BUCKETS):
        acc = jnp.where(bucket == j, rb_ref[j, h], acc)
    o_ref[:, 0] = acc


def _bias_table(rel_bias):
    bucket = _band_bucket_table()
    return pl.pallas_call(
        _bias_table_kernel,
        grid=(N_HEADS,),
        in_specs=[pl.BlockSpec(memory_space=pltpu.SMEM),
                  pl.BlockSpec(bucket.shape, lambda h: (0, 0, 0))],
        out_specs=pl.BlockSpec((2, 1, BLOCK, 2 * BLOCK), lambda h: (0, h, 0, 0)),
        out_shape=jax.ShapeDtypeStruct((2, N_HEADS, BLOCK, 2 * BLOCK), F32),
        name="bias_table",
    )(rel_bias, bucket)


def _attn_mixer_kernel(sink_ref, x_ref, g_ref, wq_ref, kvc_ref, kvp_ref, bias_ref,
                       wo_ref, o_ref, q_buf, attn_buf):
    tq = x_ref.shape[1]
    kw = N_KV_HEADS * LANES
    x = x_ref[0]
    xn = _rms(x, g_ref[...]).astype(BF16)
    q = jnp.dot(xn, wq_ref[...], preferred_element_type=F32) * (HEAD_DIM ** -0.5)
    q_buf[...] = q.astype(BF16)
    low = lax.broadcasted_iota(jnp.int32, (BLOCK, LANES), 1) < HEAD_DIM
    zero = jnp.zeros((BLOCK, LANES), BF16)

    for qb in range(tq // BLOCK):
        rows = slice(qb * BLOCK, (qb + 1) * BLOCK)
        if qb == 0:
            sel = jnp.where(pl.program_id(1) == 0, 1, 0)
        else:
            sel = 0
        for kvh in range(N_KV_HEADS):
            kl = slice(kvh * LANES, (kvh + 1) * LANES)
            vl = slice(kw + kvh * LANES, kw + (kvh + 1) * LANES)
            if qb == 0:
                k_band = jnp.concatenate([kvp_ref[0, :, kl], kvc_ref[0, 0:BLOCK, kl]], axis=0)
                v_band = jnp.concatenate([kvp_ref[0, :, vl], kvc_ref[0, 0:BLOCK, vl]], axis=0)
            else:
                band = slice((qb - 1) * BLOCK, (qb + 1) * BLOCK)
                k_band = kvc_ref[0, band, kl]
                v_band = kvc_ref[0, band, vl]
            parts = []
            for p in range(GROUP // 2):
                q2 = q_buf[rows, (kvh * 2 + p) * LANES:(kvh * 2 + p + 1) * LANES]
                parts += [jnp.where(low, q2, zero), jnp.where(low, zero, q2)]
            qs = jnp.concatenate(parts, axis=0)
            s = lax.dot_general(qs, k_band, (((1,), (1,)), ((), ())),
                                preferred_element_type=F32)
            outs = []
            for i in range(GROUP):
                head = kvh * GROUP + i
                sink = sink_ref[head]
                si = s[i * BLOCK:(i + 1) * BLOCK] + bias_ref[sel, head]
                m = jnp.maximum(jnp.max(si, axis=-1, keepdims=True), sink)
                pexp = jnp.exp(si - m)
                denom = jnp.sum(pexp, axis=-1, keepdims=True) + jnp.exp(sink - m)
                oi = jnp.dot(pexp.astype(BF16), v_band, preferred_element_type=F32)
                outs.append(oi / denom)
            for p in range(GROUP // 2):
                merged = jnp.where(low, outs[2 * p], outs[2 * p + 1])
                attn_buf[rows, (kvh * 2 + p) * LANES:(kvh * 2 + p + 1) * LANES] = merged.astype(BF16)

    o_ref[0] = x + jnp.dot(attn_buf[...], wo_ref[...], preferred_element_type=F32)


def _attn_mixer(h, g, wq, kv, bias, sinks, wo):
    B, S, D = h.shape
    tq = TQ_ATT
    nkv = kv.shape[-1]
    per = tq // BLOCK
    return pl.pallas_call(
        _attn_mixer_kernel,
        grid=(B, S // tq),
        in_specs=[
            pl.BlockSpec(memory_space=pltpu.SMEM),
            pl.BlockSpec((1, tq, D), lambda b, j: (b, j, 0)),
            _const_spec((1, D)),
            _const_spec((D, D)),
            pl.BlockSpec((1, tq, nkv), lambda b, j: (b, j, 0)),
            pl.BlockSpec((1, BLOCK, nkv), lambda b, j: (b, jnp.maximum(j * per - 1, 0), 0)),
            _const_spec(bias.shape),
            _const_spec((D, D)),
        ],
        out_specs=pl.BlockSpec((1, tq, D), lambda b, j: (b, j, 0)),
        out_shape=jax.ShapeDtypeStruct((B, S, D), F32),
        scratch_shapes=[pltpu.VMEM((tq, D), BF16), pltpu.VMEM((tq, D), BF16)],
        compiler_params=pltpu.CompilerParams(
            dimension_semantics=("arbitrary", "arbitrary"),
            vmem_limit_bytes=VMEM_LIMIT_BYTES),
        name="attn_mixer",
    )(sinks, h, g.reshape(1, D), wq.astype(BF16), kv, kv, bias, wo.astype(BF16))


def _dup_heads(w):
    d = w.shape[0]
    w = w.reshape(d, N_KV_HEADS, 1, HEAD_DIM)
    return jnp.broadcast_to(w, (d, N_KV_HEADS, LANES // HEAD_DIM, HEAD_DIM)).reshape(d, N_KV_HEADS * LANES)


def kernel(x, norm_mix, norm_ffn, conv_w_pw1, conv_b_pw1, conv_w_dw, conv_b_dw, conv_ln_g, conv_ln_b, conv_w_pw2, conv_b_pw2, norm_kv, w_kv, w_q, w_o, sinks, rel_bias, ffn_w_up, ffn_w_down, norm_final):
    B, S, D = x.shape
    kvd = N_KV_HEADS * HEAD_DIM
    w_kv2 = jnp.concatenate([_dup_heads(w_kv[:, :kvd]), _dup_heads(w_kv[:, kvd:])], axis=1).astype(BF16)
    bias = _bias_table(rel_bias)
    h = x
    kv = None
    for l in range(DEPTH):
        if l < N_A_LAYERS:
            h = _conv_mixer(h, norm_mix[l], conv_w_pw1[l], conv_b_pw1[l], conv_w_dw[l],
                            conv_b_dw[l], conv_ln_g[l], conv_ln_b[l], conv_w_pw2[l], conv_b_pw2[l])
        else:
            j = l - N_A_LAYERS
            h = _attn_mixer(h, norm_mix[l], w_q[j], kv, bias, sinks[j], w_o[j])
        h2d = h.reshape(B * S, D)
        if l == N_A_LAYERS - 1:
            h2d, kv2d = _ffn(h2d, norm_ffn[l], ffn_w_up[l], ffn_w_down[l], kv=(norm_kv, w_kv2))
            kv = kv2d.reshape(B, S, -1)
        elif l == DEPTH - 1:
            h2d = _ffn(h2d, norm_ffn[l], ffn_w_up[l], ffn_w_down[l], final_g=norm_final)
        else:
            h2d = _ffn(h2d, norm_ffn[l], ffn_w_up[l], ffn_w_down[l])
        h = h2d.reshape(B, S, D)
    return h
```

```python
import functools
import math

import jax
import jax.numpy as jnp
from jax import lax
from jax.experimental import pallas as pl
from jax.experimental.pallas import tpu as pltpu

D_MODEL = 1024
DEPTH = 4
N_A_LAYERS = DEPTH // 2
CONV_WIDTH = 31
HEAD_DIM = 64
N_HEADS = D_MODEL // HEAD_DIM
N_KV_HEADS = N_HEADS // 4
GROUP = N_HEADS // N_KV_HEADS
WINDOW = 128
BLOCK = 128
N_BUCKETS = 32
MAX_DISTANCE = 128
D_FF = -(-8 * D_MODEL // (3 * 256)) * 256
EPS = 1e-6
NEG_INF = -1e30

LANES = 128
VMEM_LIMIT_BYTES = 56 * 1024 * 1024

CONV_HALO = 32
CONV_ROWS = 64
CONV_STRIDE = 2
TM_CONV = 512
TM_FFN = 512
TF_FFN = 256
TQ_ATT = 512

F32 = jnp.float32
BF16 = jnp.bfloat16


def _rms(x, g):
    return x * lax.rsqrt(jnp.mean(x * x, axis=-1, keepdims=True) + EPS) * g


def _const_spec(shape):
    nd = len(shape)
    return pl.BlockSpec(shape, lambda *_: (0,) * nd, pipeline_mode=pl.Buffered(1))


def _layer_spec(stacked_shape, layer):
    nd = len(stacked_shape)
    return pl.BlockSpec((None,) + tuple(stacked_shape[1:]),
                        lambda *_: (layer,) + (0,) * (nd - 1), pipeline_mode=pl.Buffered(1))


def _conv_mixer_kernel(x_ref, g_ref, w1_ref, b1_ref, wdw_ref, bdw_ref, lng_ref,
                       lnb_ref, w2_ref, b2_ref, o_ref, abuf_even, abuf_odd, ybuf, xn_s):
    tm = x_ref.shape[1]
    n_slab = D_MODEL // LANES
    n_pair = n_slab // 2
    xn_s[...] = _rms(x_ref[0], g_ref[...]).astype(BF16)

    abufs = (abuf_even, abuf_odd)

    @pl.when(pl.program_id(1) == 0)
    def _():
        for buf in abufs:
            buf[:, 0:CONV_HALO, :] = jnp.zeros((n_pair, CONV_HALO, LANES), F32)

    def glu_slab(parity, idx):
        c = 2 * idx + parity
        pre = jnp.dot(xn_s[...], w1_ref[c], preferred_element_type=F32) + b1_ref[c]
        abufs[parity][idx, CONV_HALO:CONV_HALO + tm, :] = (
            pre[:, :LANES] * jax.nn.sigmoid(pre[:, LANES:]))

    first_tap = CONV_HALO - (CONV_WIDTH - 1)
    half = CONV_ROWS // CONV_STRIDE

    def conv_slab(parity, idx):
        c = 2 * idx + parity
        src = abufs[parity]
        for r0 in range(0, tm, CONV_ROWS):
            accs = [jnp.broadcast_to(bdw_ref[c], (half, LANES))] * CONV_STRIDE
            for k in range(CONV_WIDTH):
                w = wdw_ref[c, k:k + 1, :]
                for j in range(CONV_STRIDE):
                    win = src[idx, pl.ds(r0 + j + first_tap + k, half, stride=CONV_STRIDE), :]
                    accs[j] = accs[j] + win * w
            for j in range(CONV_STRIDE):
                ybuf[c, pl.ds(r0 + j, half, stride=CONV_STRIDE), :] = accs[j]

    glu_slab(0, 0)

    for t in range(n_pair):
        glu_slab(1, t)
        conv_slab(0, t)
        if t + 1 < n_pair:
            glu_slab(0, t + 1)
        conv_slab(1, t)
    for buf in abufs:
        buf[:, 0:CONV_HALO, :] = buf[:, tm:tm + CONV_HALO, :]

    y = jnp.concatenate([ybuf[c] for c in range(n_slab)], axis=-1)
    mu = jnp.mean(y, axis=-1, keepdims=True)
    yc = y - mu
    var = jnp.mean(yc * yc, axis=-1, keepdims=True)
    z = yc * lax.rsqrt(var + EPS) * lng_ref[...] + lnb_ref[...]
    z = (z * jax.nn.sigmoid(z)).astype(BF16)
    o_ref[0] = x_ref[0] + jnp.dot(z, w2_ref[...], preferred_element_type=F32) + b2_ref[...]


def _pw1_slabs(w):
    lead, rows = w.shape[:-2], w.shape[-2]
    n_slab = D_MODEL // LANES
    w = w.reshape(*lead, rows, 2, n_slab, LANES)
    w = jnp.moveaxis(w, -2, -4)
    return w.reshape(*lead, n_slab, rows, 2 * LANES)


def _conv_mixer(h, layer, g, w1s_all, b1, wdw, bdw, lng, lnb, w2_all, b2):
    B, S, D = h.shape
    tm = TM_CONV
    n_slab = D // LANES
    row = lambda v: v.reshape(1, -1)
    b1s = _pw1_slabs(b1.reshape(1, 2 * D))
    wdws = wdw.reshape(CONV_WIDTH, n_slab, LANES).transpose(1, 0, 2)
    bdws = bdw.reshape(n_slab, 1, LANES)
    return pl.pallas_call(
        _conv_mixer_kernel,
        grid=(B, S // tm),
        in_specs=[
            pl.BlockSpec((1, tm, D), lambda b, j: (b, j, 0)),
            _const_spec((1, D)),
            _layer_spec(w1s_all.shape, layer),
            _const_spec(b1s.shape),
            _const_spec(wdws.shape),
            _const_spec(bdws.shape),
            _const_spec((1, D)),
            _const_spec((1, D)),
            _layer_spec(w2_all.shape, layer),
            _const_spec((1, D)),
        ],
        out_specs=pl.BlockSpec((1, tm, D), lambda b, j: (b, j, 0)),
        out_shape=jax.ShapeDtypeStruct((B, S, D), F32),
        scratch_shapes=[
            pltpu.VMEM((n_slab // 2, CONV_HALO + tm, LANES), F32),
            pltpu.VMEM((n_slab // 2, CONV_HALO + tm, LANES), F32),
            pltpu.VMEM((n_slab, tm, LANES), F32),
            pltpu.VMEM((tm, D), BF16),
        ],
        compiler_params=pltpu.CompilerParams(
            dimension_semantics=("arbitrary", "arbitrary"),
            vmem_limit_bytes=VMEM_LIMIT_BYTES),
        name="conv_mixer",
    )(h, row(g), w1s_all, b1s, wdws, bdws, row(lng), row(lnb), w2_all, row(b2))


def _ffn_kernel(*refs, emit_kv, final_norm):
    x_ref, g_ref, wup_ref, wd_ref = refs[:4]
    rest = refs[4:]
    x = x_ref[...]
    xn = _rms(x, g_ref[...]).astype(BF16)
    acc = x
    for c in range(D_FF // TF_FFN):
        lo = c * TF_FFN
        gate = jnp.dot(xn, wup_ref[:, lo:lo + TF_FFN], preferred_element_type=F32)
        up = jnp.dot(xn, wup_ref[:, D_FF + lo:D_FF + lo + TF_FFN], preferred_element_type=F32)
        act = (gate * jax.nn.sigmoid(gate) * up).astype(BF16)
        acc = acc + jnp.dot(act, wd_ref[lo:lo + TF_FFN, :], preferred_element_type=F32)
    if emit_kv:
        gkv_ref, wkv_ref, o_ref, kv_ref = rest
        o_ref[...] = acc
        hn = _rms(acc, gkv_ref[...]).astype(BF16)
        kv_ref[...] = jnp.dot(hn, wkv_ref[...], preferred_element_type=F32).astype(BF16)
    elif final_norm:
        gf_ref, o_ref = rest
        o_ref[...] = _rms(acc, gf_ref[...])
    else:
        (o_ref,) = rest
        o_ref[...] = acc


def _ffn(h2d, layer, g, w_up_all, w_down_all, *, kv=None, final_g=None):
    M, D = h2d.shape
    tm = TM_FFN
    tile = pl.BlockSpec((tm, D), lambda i: (i, 0))
    in_specs = [tile, _const_spec((1, D)), _layer_spec(w_up_all.shape, layer),
                _layer_spec(w_down_all.shape, layer)]
    args = [h2d, g.reshape(1, D), w_up_all, w_down_all]
    out_specs = tile
    out_shape = jax.ShapeDtypeStruct((M, D), F32)
    if kv is not None:
        g_kv, w_kv2 = kv
        in_specs += [_const_spec((1, D)), _const_spec(w_kv2.shape)]
        args += [g_kv.reshape(1, D), w_kv2]
        nkv = w_kv2.shape[1]
        out_specs = [tile, pl.BlockSpec((tm, nkv), lambda i: (i, 0))]
        out_shape = [out_shape, jax.ShapeDtypeStruct((M, nkv), BF16)]
    elif final_g is not None:
        in_specs += [_const_spec((1, D))]
        args += [final_g.reshape(1, D)]
    return pl.pallas_call(
        functools.partial(_ffn_kernel, emit_kv=kv is not None,
                          final_norm=final_g is not None),
        grid=(M // tm,),
        in_specs=in_specs,
        out_specs=out_specs,
        out_shape=out_shape,
        compiler_params=pltpu.CompilerParams(
            dimension_semantics=("arbitrary",),
            vmem_limit_bytes=VMEM_LIMIT_BYTES),
        name="ffn",
    )(*args)


def _band_bucket_table():
    qi = jnp.arange(BLOCK, dtype=jnp.int32)
    kj = jnp.arange(2 * BLOCK, dtype=jnp.int32)
    dist = qi[:, None] + BLOCK - kj[None, :]
    in_window = (dist >= 0) & (dist < WINDOW)
    max_exact = N_BUCKETS // 2
    d = jnp.maximum(dist, 0)
    log_ratio = jnp.log(jnp.maximum(d, 1).astype(F32) / max_exact) / math.log(MAX_DISTANCE / max_exact)
    large = max_exact + (log_ratio * (N_BUCKETS - max_exact)).astype(jnp.int32)
    large = jnp.minimum(large, N_BUCKETS - 1)
    bucket = jnp.where(d < max_exact, d, large)
    inner = jnp.where(in_window, bucket, -1)
    first = jnp.where(in_window & (kj[None, :] >= BLOCK), bucket, -1)
    return jnp.stack([inner, first])


def _bias_table_kernel(rb_ref, bucket_ref, o_ref):
    h = pl.program_id(0)
    bucket = bucket_ref[...]
    acc = jnp.full(bucket.shape, NEG_INF, F32)
    for j in range(N_BUCKETS):
        acc = jnp.where(bucket == j, rb_ref[j, h], acc)
    o_ref[:, 0] = acc


def _bias_table(rel_bias):
    bucket = _band_bucket_table()
    return pl.pallas_call(
        _bias_table_kernel,
        grid=(N_HEADS,),
        in_specs=[pl.BlockSpec(memory_space=pltpu.SMEM),
                  pl.BlockSpec(bucket.shape, lambda h: (0, 0, 0))],
        out_specs=pl.BlockSpec((2, 1, BLOCK, 2 * BLOCK), lambda h: (0, h, 0, 0)),
        out_shape=jax.ShapeDtypeStruct((2, N_HEADS, BLOCK, 2 * BLOCK), F32),
        name="bias_table",
    )(rel_bias, bucket)


def _attn_mixer_kernel(sink_ref, x_ref, g_ref, wq_ref, kvc_ref, kvp_ref, bias_ref,
                       wo_ref, o_ref, q_buf, attn_buf):
    tq = x_ref.shape[1]
    kw = N_KV_HEADS * LANES
    x = x_ref[0]
    xn = _rms(x, g_ref[...]).astype(BF16)
    q = jnp.dot(xn, wq_ref[...], preferred_element_type=F32) * (HEAD_DIM ** -0.5)
    q_buf[...] = q.astype(BF16)
    low = lax.broadcasted_iota(jnp.int32, (BLOCK, LANES), 1) < HEAD_DIM
    zero = jnp.zeros((BLOCK, LANES), BF16)

    for qb in range(tq // BLOCK):
        rows = slice(qb * BLOCK, (qb + 1) * BLOCK)
        if qb == 0:
            sel = jnp.where(pl.program_id(1) == 0, 1, 0)
        else:
            sel = 0
        for kvh in range(N_KV_HEADS):
            kl = slice(kvh * LANES, (kvh + 1) * LANES)
            vl = slice(kw + kvh * LANES, kw + (kvh + 1) * LANES)
            if qb == 0:
                k_band = jnp.concatenate([kvp_ref[0, :, kl], kvc_ref[0, 0:BLOCK, kl]], axis=0)
                v_band = jnp.concatenate([kvp_ref[0, :, vl], kvc_ref[0, 0:BLOCK, vl]], axis=0)
            else:
                band = slice((qb - 1) * BLOCK, (qb + 1) * BLOCK)
                k_band = kvc_ref[0, band, kl]
                v_band = kvc_ref[0, band, vl]
            parts = []
            for p in range(GROUP // 2):
                q2 = q_buf[rows, (kvh * 2 + p) * LANES:(kvh * 2 + p + 1) * LANES]
                parts += [jnp.where(low, q2, zero), jnp.where(low, zero, q2)]
            qs = jnp.concatenate(parts, axis=0)
            s = lax.dot_general(qs, k_band, (((1,), (1,)), ((), ())),
                                preferred_element_type=F32)
            outs = []
            for i in range(GROUP):
                head = kvh * GROUP + i
                sink = sink_ref[head]
                si = s[i * BLOCK:(i + 1) * BLOCK] + bias_ref[sel, head]
                m = jnp.maximum(jnp.max(si, axis=-1, keepdims=True), sink)
                pexp = jnp.exp(si - m)
                denom = jnp.sum(pexp, axis=-1, keepdims=True) + jnp.exp(sink - m)
                oi = jnp.dot(pexp.astype(BF16), v_band, preferred_element_type=F32)
                outs.append(oi / denom)
            for p in range(GROUP // 2):
                merged = jnp.where(low, outs[2 * p], outs[2 * p + 1])
                attn_buf[rows, (kvh * 2 + p) * LANES:(kvh * 2 + p + 1) * LANES] = merged.astype(BF16)

    o_ref[0] = x + jnp.dot(attn_buf[...], wo_ref[...], preferred_element_type=F32)


def _attn_mixer(h, layer, g, wq_all, kv, bias, sinks, wo_all):
    B, S, D = h.shape
    tq = TQ_ATT
    nkv = kv.shape[-1]
    per = tq // BLOCK
    return pl.pallas_call(
        _attn_mixer_kernel,
        grid=(B, S // tq),
        in_specs=[
            pl.BlockSpec(memory_space=pltpu.SMEM),
            pl.BlockSpec((1, tq, D), lambda b, j: (b, j, 0)),
            _const_spec((1, D)),
            _layer_spec(wq_all.shape, layer),
            pl.BlockSpec((1, tq, nkv), lambda b, j: (b, j, 0)),
            pl.BlockSpec((1, BLOCK, nkv), lambda b, j: (b, jnp.maximum(j * per - 1, 0), 0)),
            _const_spec(bias.shape),
            _layer_spec(wo_all.shape, layer),
        ],
        out_specs=pl.BlockSpec((1, tq, D), lambda b, j: (b, j, 0)),
        out_shape=jax.ShapeDtypeStruct((B, S, D), F32),
        scratch_shapes=[pltpu.VMEM((tq, D), BF16), pltpu.VMEM((tq, D), BF16)],
        compiler_params=pltpu.CompilerParams(
            dimension_semantics=("arbitrary", "arbitrary"),
            vmem_limit_bytes=VMEM_LIMIT_BYTES),
        name="attn_mixer",
    )(sinks, h, g.reshape(1, D), wq_all, kv, kv, bias, wo_all)


def _dup_heads(w):
    d = w.shape[0]
    w = w.reshape(d, N_KV_HEADS, 1, HEAD_DIM)
    return jnp.broadcast_to(w, (d, N_KV_HEADS, LANES // HEAD_DIM, HEAD_DIM)).reshape(d, N_KV_HEADS * LANES)


def kernel(x, norm_mix, norm_ffn, conv_w_pw1, conv_b_pw1, conv_w_dw, conv_b_dw, conv_ln_g, conv_ln_b, conv_w_pw2, conv_b_pw2, norm_kv, w_kv, w_q, w_o, sinks, rel_bias, ffn_w_up, ffn_w_down, norm_final):
    B, S, D = x.shape
    kvd = N_KV_HEADS * HEAD_DIM
    w_kv2 = jnp.concatenate([_dup_heads(w_kv[:, :kvd]), _dup_heads(w_kv[:, kvd:])], axis=1).astype(BF16)
    bias = _bias_table(rel_bias)
    w1s_all = _pw1_slabs(conv_w_pw1).astype(BF16)
    w2_all = conv_w_pw2.astype(BF16)
    wq_all, wo_all = w_q.astype(BF16), w_o.astype(BF16)
    wup_all, wdown_all = ffn_w_up.astype(BF16), ffn_w_down.astype(BF16)
    h = x
    kv = None
    for l in range(DEPTH):
        if l < N_A_LAYERS:
            h = _conv_mixer(h, l, norm_mix[l], w1s_all, conv_b_pw1[l], conv_w_dw[l], conv_b_dw[l],
                            conv_ln_g[l], conv_ln_b[l], w2_all, conv_b_pw2[l])
        else:
            j = l - N_A_LAYERS
            h = _attn_mixer(h, j, norm_mix[l], wq_all, kv, bias, sinks[j], wo_all)
        h2d = h.reshape(B * S, D)
        if l == N_A_LAYERS - 1:
            h2d, kv2d = _ffn(h2d, l, norm_ffn[l], wup_all, wdown_all, kv=(norm_kv, w_kv2))
            kv = kv2d.reshape(B, S, -1)
        elif l == DEPTH - 1:
            h2d = _ffn(h2d, l, norm_ffn[l], wup_all, wdown_all, final_g=norm_final)
        else:
            h2d = _ffn(h2d, l, norm_ffn[l], wup_all, wdown_all)
        h = h2d.reshape(B, S, D)
    return h
```

```python
import functools
import math

import jax
import jax.numpy as jnp
from jax import lax
from jax.experimental import pallas as pl
from jax.experimental.pallas import tpu as pltpu

D_MODEL = 1024
DEPTH = 4
N_A_LAYERS = DEPTH // 2
CONV_WIDTH = 31
HEAD_DIM = 64
N_HEADS = D_MODEL // HEAD_DIM
N_KV_HEADS = N_HEADS // 4
GROUP = N_HEADS // N_KV_HEADS
WINDOW = 128
BLOCK = 128
N_BUCKETS = 32
MAX_DISTANCE = 128
D_FF = -(-8 * D_MODEL // (3 * 256)) * 256
EPS = 1e-6
NEG_INF = -1e30

LANES = 128
VMEM_LIMIT_BYTES = 56 * 1024 * 1024

CONV_HALO = 32
CONV_ROWS = 64
CONV_STRIDE = 2
TM_CONV = 512
TM_FFN = 512
TF_FFN = 256
TQ_ATT = 512

F32 = jnp.float32
BF16 = jnp.bfloat16


def _rms(x, g):
    return x * lax.rsqrt(jnp.mean(x * x, axis=-1, keepdims=True) + EPS) * g


def _const_spec(shape):
    nd = len(shape)
    return pl.BlockSpec(shape, lambda *_: (0,) * nd, pipeline_mode=pl.Buffered(1))


def _layer_spec(stacked_shape, layer):
    nd = len(stacked_shape)
    return pl.BlockSpec((None,) + tuple(stacked_shape[1:]),
                        lambda *_: (layer,) + (0,) * (nd - 1), pipeline_mode=pl.Buffered(1))


def _conv_mixer_kernel(x_ref, g_ref, w1_ref, b1_ref, wdw_ref, bdw_ref, lng_ref,
                       lnb_ref, w2_ref, b2_ref, o_ref, abuf_even, abuf_odd, ybuf, xn_s):
    tm = x_ref.shape[1]
    n_slab = D_MODEL // LANES
    n_pair = n_slab // 2
    xn_s[...] = _rms(x_ref[0], g_ref[...]).astype(BF16)

    abufs = (abuf_even, abuf_odd)

    @pl.when(pl.program_id(1) == 0)
    def _():
        for buf in abufs:
            buf[:, 0:CONV_HALO, :] = jnp.zeros((n_pair, CONV_HALO, LANES), F32)

    def glu_slab(parity, idx):
        c = 2 * idx + parity
        pre = jnp.dot(xn_s[...], w1_ref[c], preferred_element_type=F32) + b1_ref[c]
        abufs[parity][idx, CONV_HALO:CONV_HALO + tm, :] = (
            pre[:, :LANES] * jax.nn.sigmoid(pre[:, LANES:]))

    first_tap = CONV_HALO - (CONV_WIDTH - 1)
    half = CONV_ROWS // CONV_STRIDE

    def conv_slab(parity, idx):
        c = 2 * idx + parity
        src = abufs[parity]
        for r0 in range(0, tm, CONV_ROWS):
            accs = [jnp.broadcast_to(bdw_ref[c], (half, LANES))] * CONV_STRIDE
            for k in range(CONV_WIDTH):
                w = wdw_ref[c, k:k + 1, :]
                for j in range(CONV_STRIDE):
                    win = src[idx, pl.ds(r0 + j + first_tap + k, half, stride=CONV_STRIDE), :]
                    accs[j] = accs[j] + win * w
            for j in range(CONV_STRIDE):
                ybuf[c, pl.ds(r0 + j, half, stride=CONV_STRIDE), :] = accs[j]

    glu_slab(0, 0)

    for t in range(n_pair):
        glu_slab(1, t)
        conv_slab(0, t)
        if t + 1 < n_pair:
            glu_slab(0, t + 1)
        conv_slab(1, t)
    for buf in abufs:
        buf[:, 0:CONV_HALO, :] = buf[:, tm:tm + CONV_HALO, :]

    y = jnp.concatenate([ybuf[c] for c in range(n_slab)], axis=-1)
    mu = jnp.mean(y, axis=-1, keepdims=True)
    yc = y - mu
    var = jnp.mean(yc * yc, axis=-1, keepdims=True)
    z = yc * lax.rsqrt(var + EPS) * lng_ref[...] + lnb_ref[...]
    z = (z * jax.nn.sigmoid(z)).astype(BF16)
    o_ref[0] = x_ref[0] + jnp.dot(z, w2_ref[...], preferred_element_type=F32) + b2_ref[...]


def _pw1_slabs(w):
    lead, rows = w.shape[:-2], w.shape[-2]
    n_slab = D_MODEL // LANES
    w = w.reshape(*lead, rows, 2, n_slab, LANES)
    w = jnp.moveaxis(w, -2, -4)
    return w.reshape(*lead, n_slab, rows, 2 * LANES)


def _conv_mixer(h, layer, g, w1s_all, b1, wdw, bdw, lng, lnb, w2_all, b2):
    B, S, D = h.shape
    tm = TM_CONV
    n_slab = D // LANES
    row = lambda v: v.reshape(1, -1)
    b1s = _pw1_slabs(b1.reshape(1, 2 * D))
    wdws = wdw.reshape(CONV_WIDTH, n_slab, LANES).transpose(1, 0, 2)
    bdws = bdw.reshape(n_slab, 1, LANES)
    return pl.pallas_call(
        _conv_mixer_kernel,
        grid=(B, S // tm),
        in_specs=[
            pl.BlockSpec((1, tm, D), lambda b, j: (b, j, 0)),
            _const_spec((1, D)),
            _layer_spec(w1s_all.shape, layer),
            _const_spec(b1s.shape),
            _const_spec(wdws.shape),
            _const_spec(bdws.shape),
            _const_spec((1, D)),
            _const_spec((1, D)),
            _layer_spec(w2_all.shape, layer),
            _const_spec((1, D)),
        ],
        out_specs=pl.BlockSpec((1, tm, D), lambda b, j: (b, j, 0)),
        out_shape=jax.ShapeDtypeStruct((B, S, D), F32),
        scratch_shapes=[
            pltpu.VMEM((n_slab // 2, CONV_HALO + tm, LANES), F32),
            pltpu.VMEM((n_slab // 2, CONV_HALO + tm, LANES), F32),
            pltpu.VMEM((n_slab, tm, LANES), F32),
            pltpu.VMEM((tm, D), BF16),
        ],
        compiler_params=pltpu.CompilerParams(
            dimension_semantics=("arbitrary", "arbitrary"),
            vmem_limit_bytes=VMEM_LIMIT_BYTES),
        name="conv_mixer",
    )(h, row(g), w1s_all, b1s, wdws, bdws, row(lng), row(lnb), w2_all, row(b2))


def _ffn_kernel(*refs, emit_kv, final_norm):
    x_ref, g_ref, wup_ref, wd_ref = refs[:4]
    rest = refs[4:]
    x = x_ref[...]
    xn = _rms(x, g_ref[...]).astype(BF16)
    acc = x
    for c in range(D_FF // TF_FFN):
        lo = c * TF_FFN
        gate = jnp.dot(xn, wup_ref[:, lo:lo + TF_FFN], preferred_element_type=F32)
        up = jnp.dot(xn, wup_ref[:, D_FF + lo:D_FF + lo + TF_FFN], preferred_element_type=F32)
        act = (gate * jax.nn.sigmoid(gate) * up).astype(BF16)
        acc = acc + jnp.dot(act, wd_ref[lo:lo + TF_FFN, :], preferred_element_type=F32)
    if emit_kv:
        gkv_ref, wk_ref, wvt_ref, o_ref, k_ref, vt_ref = rest
        o_ref[...] = acc
        hn = _rms(acc, gkv_ref[...]).astype(BF16)
        k_ref[...] = jnp.dot(hn, wk_ref[...], preferred_element_type=F32).astype(BF16)
        vt_ref[...] = lax.dot_general(wvt_ref[...], hn, (((1,), (1,)), ((), ())),
                                      preferred_element_type=F32).astype(BF16)
    elif final_norm:
        gf_ref, o_ref = rest
        o_ref[...] = _rms(acc, gf_ref[...])
    else:
        (o_ref,) = rest
        o_ref[...] = acc


def _ffn(h2d, layer, g, w_up_all, w_down_all, *, kv=None, final_g=None):
    M, D = h2d.shape
    tm = TM_FFN
    tile = pl.BlockSpec((tm, D), lambda i: (i, 0))
    in_specs = [tile, _const_spec((1, D)), _layer_spec(w_up_all.shape, layer),
                _layer_spec(w_down_all.shape, layer)]
    args = [h2d, g.reshape(1, D), w_up_all, w_down_all]
    out_specs = tile
    out_shape = jax.ShapeDtypeStruct((M, D), F32)
    if kv is not None:
        g_kv, w_k, w_vt = kv
        in_specs += [_const_spec((1, D)), _const_spec(w_k.shape), _const_spec(w_vt.shape)]
        args += [g_kv.reshape(1, D), w_k, w_vt]
        nk, nv = w_k.shape[1], w_vt.shape[0]
        out_specs = [tile, pl.BlockSpec((tm, nk), lambda i: (i, 0)),
                     pl.BlockSpec((nv, tm), lambda i: (0, i))]
        out_shape = [out_shape, jax.ShapeDtypeStruct((M, nk), BF16),
                     jax.ShapeDtypeStruct((nv, M), BF16)]
    elif final_g is not None:
        in_specs += [_const_spec((1, D))]
        args += [final_g.reshape(1, D)]
    return pl.pallas_call(
        functools.partial(_ffn_kernel, emit_kv=kv is not None,
                          final_norm=final_g is not None),
        grid=(M // tm,),
        in_specs=in_specs,
        out_specs=out_specs,
        out_shape=out_shape,
        compiler_params=pltpu.CompilerParams(
            dimension_semantics=("arbitrary",),
            vmem_limit_bytes=VMEM_LIMIT_BYTES),
        name="ffn",
    )(*args)


def _band_bucket_table():
    qi = jnp.arange(BLOCK, dtype=jnp.int32)
    kj = jnp.arange(2 * BLOCK, dtype=jnp.int32)
    dist = qi[:, None] + BLOCK - kj[None, :]
    in_window = (dist >= 0) & (dist < WINDOW)
    max_exact = N_BUCKETS // 2
    d = jnp.maximum(dist, 0)
    log_ratio = jnp.log(jnp.maximum(d, 1).astype(F32) / max_exact) / math.log(MAX_DISTANCE / max_exact)
    large = max_exact + (log_ratio * (N_BUCKETS - max_exact)).astype(jnp.int32)
    large = jnp.minimum(large, N_BUCKETS - 1)
    bucket = jnp.where(d < max_exact, d, large)
    inner = jnp.where(in_window, bucket, -1)
    first = jnp.where(in_window & (kj[None, :] >= BLOCK), bucket, -1)
    return jnp.stack([inner.T, first.T])


def _bias_table_kernel(rb_ref, bucket_ref, o_ref):
    h = pl.program_id(0)
    bucket = bucket_ref[...]
    acc = jnp.full(bucket.shape, NEG_INF, F32)
    for j in range(N_BUCKETS):
        acc = jnp.where(bucket == j, rb_ref[j, h], acc)
    o_ref[:, 0] = acc


def _bias_table(rel_bias):
    bucket = _band_bucket_table()
    return pl.pallas_call(
        _bias_table_kernel,
        grid=(N_HEADS,),
        in_specs=[pl.BlockSpec(memory_space=pltpu.SMEM),
                  pl.BlockSpec(bucket.shape, lambda h: (0, 0, 0))],
        out_specs=pl.BlockSpec((2, 1, 2 * BLOCK, BLOCK), lambda h: (0, h // GROUP, 0, h % GROUP)),
        out_shape=jax.ShapeDtypeStruct((2, N_KV_HEADS, 2 * BLOCK, GROUP * BLOCK), F32),
        name="bias_table",
    )(rel_bias, bucket)


def _attn_mixer_kernel(sink_ref, x_ref, g_ref, wqt_ref, kc_ref, kp_ref, vtc_ref, vtp_ref,
                       bias_ref, wo_ref, o_ref, qt_buf, attnt_buf, *, tiles_per_seq):
    tq = x_ref.shape[0]
    x = x_ref[...]
    xn = _rms(x, g_ref[...]).astype(BF16)
    qt = lax.dot_general(wqt_ref[...], xn, (((1,), (1,)), ((), ())),
                         preferred_element_type=F32)
    qt_buf[...] = (qt * (HEAD_DIM ** -0.5)).astype(BF16)
    first_tile = lax.rem(pl.program_id(0), tiles_per_seq) == 0

    n_qb = tq // BLOCK
    group_heads = [[kvh * GROUP + i for i in range(GROUP)] for kvh in range(N_KV_HEADS)]

    def scores(qb):
        ql = slice(qb * BLOCK, (qb + 1) * BLOCK)
        sel = jnp.where(first_tile, 1, 0) if qb == 0 else 0
        out = []
        for kvh, heads in enumerate(group_heads):
            kl = slice(kvh * LANES, kvh * LANES + HEAD_DIM)
            if qb == 0:
                k_band = jnp.concatenate([kp_ref[:, kl], kc_ref[0:BLOCK, kl]], axis=0)
            else:
                k_band = kc_ref[(qb - 1) * BLOCK:(qb + 1) * BLOCK, kl]
            q4 = jnp.concatenate([qt_buf[h * HEAD_DIM:(h + 1) * HEAD_DIM, ql] for h in heads],
                                 axis=1)
            out.append(jnp.dot(k_band, q4, preferred_element_type=F32) + bias_ref[sel, kvh])
        return out

    def attend(qb, sts):
        ql = slice(qb * BLOCK, (qb + 1) * BLOCK)
        for kvh, heads in enumerate(group_heads):
            vr = slice(kvh * HEAD_DIM, (kvh + 1) * HEAD_DIM)
            if qb == 0:
                vt_band = jnp.concatenate([vtp_ref[vr, :], vtc_ref[vr, 0:BLOCK]], axis=1)
            else:
                vt_band = vtc_ref[vr, (qb - 1) * BLOCK:(qb + 1) * BLOCK]
            st = sts[kvh]
            sink = jnp.concatenate([jnp.full((1, BLOCK), sink_ref[h], F32) for h in heads], axis=1)
            m = jnp.maximum(jnp.max(st, axis=0, keepdims=True), sink)
            pexp = jnp.exp(st - m)
            denom = jnp.sum(pexp, axis=0, keepdims=True) + jnp.exp(sink - m)
            ot = jnp.dot(vt_band, pexp.astype(BF16), preferred_element_type=F32) / denom
            for i, h in enumerate(heads):
                attnt_buf[h * HEAD_DIM:(h + 1) * HEAD_DIM, ql] = (
                    ot[:, i * BLOCK:(i + 1) * BLOCK].astype(BF16))

    sts = scores(0)
    for qb in range(n_qb):
        nxt = scores(qb + 1) if qb + 1 < n_qb else None
        attend(qb, sts)
        sts = nxt

    out = lax.dot_general(attnt_buf[...], wo_ref[...], (((0,), (0,)), ((), ())),
                          preferred_element_type=F32)
    o_ref[...] = x + out


def _attn_mixer(h2d, seq_len, layer, g, wqt_all, k2d, vt2d, bias, sinks, wo_all):
    M, D = h2d.shape
    tq = TQ_ATT
    per = tq // BLOCK
    nk, nv = k2d.shape[1], vt2d.shape[0]
    assert seq_len % tq == 0
    prev = lambda i: jnp.maximum(i * per - 1, 0)
    return pl.pallas_call(
        functools.partial(_attn_mixer_kernel, tiles_per_seq=seq_len // tq),
        grid=(M // tq,),
        in_specs=[
            pl.BlockSpec(memory_space=pltpu.SMEM),
            pl.BlockSpec((tq, D), lambda i: (i, 0)),
            _const_spec((1, D)),
            _layer_spec(wqt_all.shape, layer),
            pl.BlockSpec((tq, nk), lambda i: (i, 0)),
            pl.BlockSpec((BLOCK, nk), lambda i: (prev(i), 0)),
            pl.BlockSpec((nv, tq), lambda i: (0, i)),
            pl.BlockSpec((nv, BLOCK), lambda i: (0, prev(i))),
            _const_spec(bias.shape),
            _layer_spec(wo_all.shape, layer),
        ],
        out_specs=pl.BlockSpec((tq, D), lambda i: (i, 0)),
        out_shape=jax.ShapeDtypeStruct((M, D), F32),
        scratch_shapes=[pltpu.VMEM((D, tq), BF16), pltpu.VMEM((D, tq), BF16)],
        compiler_params=pltpu.CompilerParams(
            dimension_semantics=("arbitrary",),
            vmem_limit_bytes=VMEM_LIMIT_BYTES),
        name="attn_mixer",
    )(sinks, h2d, g.reshape(1, D), wqt_all, k2d, k2d, vt2d, vt2d, bias, wo_all)


def kernel(x, norm_mix, norm_ffn, conv_w_pw1, conv_b_pw1, conv_w_dw, conv_b_dw, conv_ln_g, conv_ln_b, conv_w_pw2, conv_b_pw2, norm_kv, w_kv, w_q, w_o, sinks, rel_bias, ffn_w_up, ffn_w_down, norm_final):
    B, S, D = x.shape
    kvd = N_KV_HEADS * HEAD_DIM
    w_k = jnp.pad(w_kv[:, :kvd].reshape(D, N_KV_HEADS, HEAD_DIM),
                  ((0, 0), (0, 0), (0, LANES - HEAD_DIM))).reshape(D, N_KV_HEADS * LANES).astype(BF16)
    w_vt = w_kv[:, kvd:].T.astype(BF16)
    bias = _bias_table(rel_bias)
    w1s_all = _pw1_slabs(conv_w_pw1).astype(BF16)
    w2_all = conv_w_pw2.astype(BF16)
    wqt_all, wo_all = jnp.swapaxes(w_q, 1, 2).astype(BF16), w_o.astype(BF16)
    wup_all, wdown_all = ffn_w_up.astype(BF16), ffn_w_down.astype(BF16)
    h = x
    k2d = vt2d = None
    for l in range(DEPTH):
        if l < N_A_LAYERS:
            h = _conv_mixer(h, l, norm_mix[l], w1s_all, conv_b_pw1[l], conv_w_dw[l], conv_b_dw[l],
                            conv_ln_g[l], conv_ln_b[l], w2_all, conv_b_pw2[l])
            h2d = h.reshape(B * S, D)
        else:
            j = l - N_A_LAYERS
            h2d = _attn_mixer(h2d, S, j, norm_mix[l], wqt_all, k2d, vt2d, bias, sinks[j], wo_all)
        if l == N_A_LAYERS - 1:
            h2d, k2d, vt2d = _ffn(h2d, l, norm_ffn[l], wup_all, wdown_all, kv=(norm_kv, w_k, w_vt))
        elif l == DEPTH - 1:
            h2d = _ffn(h2d, l, norm_ffn[l], wup_all, wdown_all, final_g=norm_final)
        else:
            h2d = _ffn(h2d, l, norm_ffn[l], wup_all, wdown_all)
        h = h2d.reshape(B, S, D)
    return h
```

```python
import functools
import math

import jax
import jax.numpy as jnp
from jax import lax
from jax.experimental import pallas as pl
from jax.experimental.pallas import tpu as pltpu

D_MODEL = 1024
DEPTH = 4
N_A_LAYERS = DEPTH // 2
CONV_WIDTH = 31
HEAD_DIM = 64
N_HEADS = D_MODEL // HEAD_DIM
N_KV_HEADS = N_HEADS // 4
GROUP = N_HEADS // N_KV_HEADS
WINDOW = 128
BLOCK = 128
N_BUCKETS = 32
MAX_DISTANCE = 128
D_FF = -(-8 * D_MODEL // (3 * 256)) * 256
EPS = 1e-6
NEG_INF = -1e30

LANES = 128
VMEM_LIMIT_BYTES = 56 * 1024 * 1024

CONV_HALO = 32
CONV_ROWS = 64
CONV_STRIDE = 2
TM_CONV = 512
TM_FFN = 512
TF_FFN = 256
TQ_ATT = 512

F32 = jnp.float32
BF16 = jnp.bfloat16


def _rms(x, g):
    return x * lax.rsqrt(jnp.mean(x * x, axis=-1, keepdims=True) + EPS) * g


def _const_spec(shape):
    nd = len(shape)
    return pl.BlockSpec(shape, lambda *_: (0,) * nd, pipeline_mode=pl.Buffered(1))


def _layer_spec(stacked_shape, layer):
    nd = len(stacked_shape)
    return pl.BlockSpec((None,) + tuple(stacked_shape[1:]),
                        lambda *_: (layer,) + (0,) * (nd - 1), pipeline_mode=pl.Buffered(1))


def _conv_mixer_kernel(x_ref, xp_ref, g_ref, w1_ref, b1_ref, wdw_ref, bdw_ref, lng_ref,
                       lnb_ref, w2_ref, b2_ref, o_ref, abuf_even, abuf_odd, ybuf, xn_s, z_s,
                       *, tiles_per_seq):
    i = pl.program_id(0)
    tm = x_ref.shape[0]
    n_slab = D_MODEL // LANES
    n_pair = n_slab // 2
    pw2_cols = D_MODEL // n_pair

    abufs = (abuf_even, abuf_odd)

    @pl.when(i == 0)
    def _():
        ybuf[...] = jnp.zeros(ybuf.shape, F32)

    @pl.when(lax.rem(i, tiles_per_seq) == 0)
    def _():
        for buf in abufs:
            buf[:, 0:CONV_HALO, :] = jnp.zeros((n_pair, CONV_HALO, LANES), F32)

    y = jnp.concatenate([ybuf[c] for c in range(n_slab)], axis=-1)
    mu = jnp.mean(y, axis=-1, keepdims=True)
    yc = y - mu
    var = jnp.mean(yc * yc, axis=-1, keepdims=True)
    z = yc * lax.rsqrt(var + EPS) * lng_ref[...] + lnb_ref[...]
    z_s[...] = (z * jax.nn.sigmoid(z)).astype(BF16)

    xn_s[...] = _rms(x_ref[...], g_ref[...]).astype(BF16)

    def glu_slab(parity, idx):
        c = 2 * idx + parity
        pre = jnp.dot(xn_s[...], w1_ref[c], preferred_element_type=F32) + b1_ref[c]
        abufs[parity][idx, CONV_HALO:CONV_HALO + tm, :] = (
            pre[:, :LANES] * jax.nn.sigmoid(pre[:, LANES:]))

    first_tap = CONV_HALO - (CONV_WIDTH - 1)
    half = CONV_ROWS // CONV_STRIDE

    def conv_slab(parity, idx):
        c = 2 * idx + parity
        src = abufs[parity]
        for r0 in range(0, tm, CONV_ROWS):
            accs = [jnp.broadcast_to(bdw_ref[c], (half, LANES))] * CONV_STRIDE
            for k in range(CONV_WIDTH):
                w = wdw_ref[c, k:k + 1, :]
                for j in range(CONV_STRIDE):
                    win = src[idx, pl.ds(r0 + j + first_tap + k, half, stride=CONV_STRIDE), :]
                    accs[j] = accs[j] + win * w
            for j in range(CONV_STRIDE):
                ybuf[c, pl.ds(r0 + j, half, stride=CONV_STRIDE), :] = accs[j]

    def pw2_piece(t):
        cs = slice(t * pw2_cols, (t + 1) * pw2_cols)
        o_ref[:, cs] = (xp_ref[:, cs] + b2_ref[:, cs]
                        + jnp.dot(z_s[...], w2_ref[:, cs], preferred_element_type=F32))

    for t in range(n_pair):
        glu_slab(0, t)
        glu_slab(1, t)
    for t in range(n_pair):
        pw2_piece(t)
    for c in range(n_slab):
        conv_slab(c % 2, c // 2)
    for buf in abufs:
        buf[:, 0:CONV_HALO, :] = buf[:, tm:tm + CONV_HALO, :]


def _pw1_slabs(w):
    lead, rows = w.shape[:-2], w.shape[-2]
    n_slab = D_MODEL // LANES
    w = w.reshape(*lead, rows, 2, n_slab, LANES)
    w = jnp.moveaxis(w, -2, -4)
    return w.reshape(*lead, n_slab, rows, 2 * LANES)


def _conv_mixer(h, layer, g, w1s_all, b1, wdw, bdw, lng, lnb, w2_all, b2):
    B, S, D = h.shape
    tm = TM_CONV
    n_slab = D // LANES
    nt = B * S // tm
    assert S % tm == 0
    row = lambda v: v.reshape(1, -1)
    b1s = _pw1_slabs(b1.reshape(1, 2 * D))
    wdws = wdw.reshape(CONV_WIDTH, n_slab, LANES).transpose(1, 0, 2)
    bdws = bdw.reshape(n_slab, 1, LANES)
    h2d = h.reshape(B * S, D)
    prev_tile = pl.BlockSpec((tm, D), lambda i: (jnp.maximum(i - 1, 0), 0))
    out = pl.pallas_call(
        functools.partial(_conv_mixer_kernel, tiles_per_seq=S // tm),
        grid=(nt + 1,),
        in_specs=[
            pl.BlockSpec((tm, D), lambda i: (jnp.minimum(i, nt - 1), 0)),
            prev_tile,
            _const_spec((1, D)),
            _layer_spec(w1s_all.shape, layer),
            _const_spec(b1s.shape),
            _const_spec(wdws.shape),
            _const_spec(bdws.shape),
            _const_spec((1, D)),
            _const_spec((1, D)),
            _layer_spec(w2_all.shape, layer),
            _const_spec((1, D)),
        ],
        out_specs=prev_tile,
        out_shape=jax.ShapeDtypeStruct((B * S, D), F32),
        scratch_shapes=[
            pltpu.VMEM((n_slab // 2, CONV_HALO + tm, LANES), F32),
            pltpu.VMEM((n_slab // 2, CONV_HALO + tm, LANES), F32),
            pltpu.VMEM((n_slab, tm, LANES), F32),
            pltpu.VMEM((tm, D), BF16),
            pltpu.VMEM((tm, D), BF16),
        ],
        compiler_params=pltpu.CompilerParams(
            dimension_semantics=("arbitrary",),
            vmem_limit_bytes=VMEM_LIMIT_BYTES),
        name="conv_mixer",
    )(h2d, h2d, row(g), w1s_all, b1s, wdws, bdws, row(lng), row(lnb), w2_all, row(b2))
    return out.reshape(B, S, D)


def _ffn_kernel(*refs, emit_kv, final_norm):
    x_ref, g_ref, wup_ref, wd_ref = refs[:4]
    rest = refs[4:]
    x = x_ref[...]
    xn = _rms(x, g_ref[...]).astype(BF16)
    acc = x
    for c in range(D_FF // TF_FFN):
        lo = c * TF_FFN
        gate = jnp.dot(xn, wup_ref[:, lo:lo + TF_FFN], preferred_element_type=F32)
        up = jnp.dot(xn, wup_ref[:, D_FF + lo:D_FF + lo + TF_FFN], preferred_element_type=F32)
        act = (gate * jax.nn.sigmoid(gate) * up).astype(BF16)
        acc = acc + jnp.dot(act, wd_ref[lo:lo + TF_FFN, :], preferred_element_type=F32)
    if emit_kv:
        gkv_ref, wk_ref, wvt_ref, o_ref, k_ref, vt_ref = rest
        o_ref[...] = acc
        hn = _rms(acc, gkv_ref[...]).astype(BF16)
        k_ref[...] = jnp.dot(hn, wk_ref[...], preferred_element_type=F32).astype(BF16)
        vt_ref[...] = lax.dot_general(wvt_ref[...], hn, (((1,), (1,)), ((), ())),
                                      preferred_element_type=F32).astype(BF16)
    elif final_norm:
        gf_ref, o_ref = rest
        o_ref[...] = _rms(acc, gf_ref[...])
    else:
        (o_ref,) = rest
        o_ref[...] = acc


def _ffn(h2d, layer, g, w_up_all, w_down_all, *, kv=None, final_g=None):
    M, D = h2d.shape
    tm = TM_FFN
    tile = pl.BlockSpec((tm, D), lambda i: (i, 0))
    in_specs = [tile, _const_spec((1, D)), _layer_spec(w_up_all.shape, layer),
                _layer_spec(w_down_all.shape, layer)]
    args = [h2d, g.reshape(1, D), w_up_all, w_down_all]
    out_specs = tile
    out_shape = jax.ShapeDtypeStruct((M, D), F32)
    if kv is not None:
        g_kv, w_k, w_vt = kv
        in_specs += [_const_spec((1, D)), _const_spec(w_k.shape), _const_spec(w_vt.shape)]
        args += [g_kv.reshape(1, D), w_k, w_vt]
        nk, nv = w_k.shape[1], w_vt.shape[0]
        out_specs = [tile, pl.BlockSpec((tm, nk), lambda i: (i, 0)),
                     pl.BlockSpec((nv, tm), lambda i: (0, i))]
        out_shape = [out_shape, jax.ShapeDtypeStruct((M, nk), BF16),
                     jax.ShapeDtypeStruct((nv, M), BF16)]
    elif final_g is not None:
        in_specs += [_const_spec((1, D))]
        args += [final_g.reshape(1, D)]
    return pl.pallas_call(
        functools.partial(_ffn_kernel, emit_kv=kv is not None,
                          final_norm=final_g is not None),
        grid=(M // tm,),
        in_specs=in_specs,
        out_specs=out_specs,
        out_shape=out_shape,
        compiler_params=pltpu.CompilerParams(
            dimension_semantics=("arbitrary",),
            vmem_limit_bytes=VMEM_LIMIT_BYTES),
        name="ffn",
    )(*args)


def _band_bucket_table():
    qi = jnp.arange(BLOCK, dtype=jnp.int32)
    kj = jnp.arange(2 * BLOCK, dtype=jnp.int32)
    dist = qi[:, None] + BLOCK - kj[None, :]
    in_window = (dist >= 0) & (dist < WINDOW)
    max_exact = N_BUCKETS // 2
    d = jnp.maximum(dist, 0)
    log_ratio = jnp.log(jnp.maximum(d, 1).astype(F32) / max_exact) / math.log(MAX_DISTANCE / max_exact)
    large = max_exact + (log_ratio * (N_BUCKETS - max_exact)).astype(jnp.int32)
    large = jnp.minimum(large, N_BUCKETS - 1)
    bucket = jnp.where(d < max_exact, d, large)
    inner = jnp.where(in_window, bucket, -1)
    first = jnp.where(in_window & (kj[None, :] >= BLOCK), bucket, -1)
    return jnp.stack([inner.T, first.T])


def _bias_table_kernel(rb_ref, bucket_ref, o_ref):
    h = pl.program_id(0)
    bucket = bucket_ref[...]
    acc = jnp.full(bucket.shape, NEG_INF, F32)
    for j in range(N_BUCKETS):
        acc = jnp.where(bucket == j, rb_ref[j, h], acc)
    o_ref[:, 0] = acc


def _bias_table(rel_bias):
    bucket = _band_bucket_table()
    return pl.pallas_call(
        _bias_table_kernel,
        grid=(N_HEADS,),
        in_specs=[pl.BlockSpec(memory_space=pltpu.SMEM),
                  pl.BlockSpec(bucket.shape, lambda h: (0, 0, 0))],
        out_specs=pl.BlockSpec((2, 1, 2 * BLOCK, BLOCK), lambda h: (0, h // GROUP, 0, h % GROUP)),
        out_shape=jax.ShapeDtypeStruct((2, N_KV_HEADS, 2 * BLOCK, GROUP * BLOCK), F32),
        name="bias_table",
    )(rel_bias, bucket)


def _attn_mixer_kernel(sink_ref, x_ref, g_ref, wqt_ref, kc_ref, kp_ref, vtc_ref, vtp_ref,
                       bias_ref, wo_ref, o_ref, qt_buf, attnt_buf, *, tiles_per_seq):
    tq = x_ref.shape[0]
    x = x_ref[...]
    xn = _rms(x, g_ref[...]).astype(BF16)
    qt = lax.dot_general(wqt_ref[...], xn, (((1,), (1,)), ((), ())),
                         preferred_element_type=F32)
    qt_buf[...] = (qt * (HEAD_DIM ** -0.5)).astype(BF16)
    first_tile = lax.rem(pl.program_id(0), tiles_per_seq) == 0

    n_qb = tq // BLOCK
    group_heads = [[kvh * GROUP + i for i in range(GROUP)] for kvh in range(N_KV_HEADS)]

    def scores(qb):
        ql = slice(qb * BLOCK, (qb + 1) * BLOCK)
        sel = jnp.where(first_tile, 1, 0) if qb == 0 else 0
        out = []
        for kvh, heads in enumerate(group_heads):
            kl = slice(kvh * LANES, kvh * LANES + HEAD_DIM)
            if qb == 0:
                k_band = jnp.concatenate([kp_ref[:, kl], kc_ref[0:BLOCK, kl]], axis=0)
            else:
                k_band = kc_ref[(qb - 1) * BLOCK:(qb + 1) * BLOCK, kl]
            q4 = jnp.concatenate([qt_buf[h * HEAD_DIM:(h + 1) * HEAD_DIM, ql] for h in heads],
                                 axis=1)
            out.append(jnp.dot(k_band, q4, preferred_element_type=F32) + bias_ref[sel, kvh])
        return out

    def attend(qb, sts):
        ql = slice(qb * BLOCK, (qb + 1) * BLOCK)
        for kvh, heads in enumerate(group_heads):
            vr = slice(kvh * HEAD_DIM, (kvh + 1) * HEAD_DIM)
            if qb == 0:
                vt_band = jnp.concatenate([vtp_ref[vr, :], vtc_ref[vr, 0:BLOCK]], axis=1)
            else:
                vt_band = vtc_ref[vr, (qb - 1) * BLOCK:(qb + 1) * BLOCK]
            st = sts[kvh]
            sink = jnp.concatenate([jnp.full((1, BLOCK), sink_ref[h], F32) for h in heads], axis=1)
            m = jnp.maximum(jnp.max(st, axis=0, keepdims=True), sink)
            pexp = jnp.exp(st - m)
            denom = jnp.sum(pexp, axis=0, keepdims=True) + jnp.exp(sink - m)
            ot = jnp.dot(vt_band, pexp.astype(BF16), preferred_element_type=F32) / denom
            for i, h in enumerate(heads):
                attnt_buf[h * HEAD_DIM:(h + 1) * HEAD_DIM, ql] = (
                    ot[:, i * BLOCK:(i + 1) * BLOCK].astype(BF16))

    sts = scores(0)
    for qb in range(n_qb):
        nxt = scores(qb + 1) if qb + 1 < n_qb else None
        attend(qb, sts)
        sts = nxt

    out = lax.dot_general(attnt_buf[...], wo_ref[...], (((0,), (0,)), ((), ())),
                          preferred_element_type=F32)
    o_ref[...] = x + out


def _attn_mixer(h2d, seq_len, layer, g, wqt_all, k2d, vt2d, bias, sinks, wo_all):
    M, D = h2d.shape
    tq = TQ_ATT
    per = tq // BLOCK
    nk, nv = k2d.shape[1], vt2d.shape[0]
    assert seq_len % tq == 0
    prev = lambda i: jnp.maximum(i * per - 1, 0)
    return pl.pallas_call(
        functools.partial(_attn_mixer_kernel, tiles_per_seq=seq_len // tq),
        grid=(M // tq,),
        in_specs=[
            pl.BlockSpec(memory_space=pltpu.SMEM),
            pl.BlockSpec((tq, D), lambda i: (i, 0)),
            _const_spec((1, D)),
            _layer_spec(wqt_all.shape, layer),
            pl.BlockSpec((tq, nk), lambda i: (i, 0)),
            pl.BlockSpec((BLOCK, nk), lambda i: (prev(i), 0)),
            pl.BlockSpec((nv, tq), lambda i: (0, i)),
            pl.BlockSpec((nv, BLOCK), lambda i: (0, prev(i))),
            _const_spec(bias.shape),
            _layer_spec(wo_all.shape, layer),
        ],
        out_specs=pl.BlockSpec((tq, D), lambda i: (i, 0)),
        out_shape=jax.ShapeDtypeStruct((M, D), F32),
        scratch_shapes=[pltpu.VMEM((D, tq), BF16), pltpu.VMEM((D, tq), BF16)],
        compiler_params=pltpu.CompilerParams(
            dimension_semantics=("arbitrary",),
            vmem_limit_bytes=VMEM_LIMIT_BYTES),
        name="attn_mixer",
    )(sinks, h2d, g.reshape(1, D), wqt_all, k2d, k2d, vt2d, vt2d, bias, wo_all)


def kernel(x, norm_mix, norm_ffn, conv_w_pw1, conv_b_pw1, conv_w_dw, conv_b_dw, conv_ln_g, conv_ln_b, conv_w_pw2, conv_b_pw2, norm_kv, w_kv, w_q, w_o, sinks, rel_bias, ffn_w_up, ffn_w_down, norm_final):
    B, S, D = x.shape
    kvd = N_KV_HEADS * HEAD_DIM
    w_k = jnp.pad(w_kv[:, :kvd].reshape(D, N_KV_HEADS, HEAD_DIM),
                  ((0, 0), (0, 0), (0, LANES - HEAD_DIM))).reshape(D, N_KV_HEADS * LANES).astype(BF16)
    w_vt = w_kv[:, kvd:].T.astype(BF16)
    bias = _bias_table(rel_bias)
    w1s_all = _pw1_slabs(conv_w_pw1).astype(BF16)
    w2_all = conv_w_pw2.astype(BF16)
    wqt_all, wo_all = jnp.swapaxes(w_q, 1, 2).astype(BF16), w_o.astype(BF16)
    wup_all, wdown_all = ffn_w_up.astype(BF16), ffn_w_down.astype(BF16)
    h = x
    k2d = vt2d = None
    for l in range(DEPTH):
        if l < N_A_LAYERS:
            h = _conv_mixer(h, l, norm_mix[l], w1s_all, conv_b_pw1[l], conv_w_dw[l], conv_b_dw[l],
                            conv_ln_g[l], conv_ln_b[l], w2_all, conv_b_pw2[l])
            h2d = h.reshape(B * S, D)
        else:
            j = l - N_A_LAYERS
            h2d = _attn_mixer(h2d, S, j, norm_mix[l], wqt_all, k2d, vt2d, bias, sinks[j], wo_all)
        if l == N_A_LAYERS - 1:
            h2d, k2d, vt2d = _ffn(h2d, l, norm_ffn[l], wup_all, wdown_all, kv=(norm_kv, w_k, w_vt))
        elif l == DEPTH - 1:
            h2d = _ffn(h2d, l, norm_ffn[l], wup_all, wdown_all, final_g=norm_final)
        else:
            h2d = _ffn(h2d, l, norm_ffn[l], wup_all, wdown_all)
        h = h2d.reshape(B, S, D)
    return h
```

```python
import functools
import math

import jax
import jax.numpy as jnp
from jax import lax
from jax.experimental import pallas as pl
from jax.experimental.pallas import tpu as pltpu

D_MODEL = 1024
DEPTH = 4
N_A_LAYERS = DEPTH // 2
CONV_WIDTH = 31
HEAD_DIM = 64
N_HEADS = D_MODEL // HEAD_DIM
N_KV_HEADS = N_HEADS // 4
GROUP = N_HEADS // N_KV_HEADS
WINDOW = 128
BLOCK = 128
N_BUCKETS = 32
MAX_DISTANCE = 128
D_FF = -(-8 * D_MODEL // (3 * 256)) * 256
EPS = 1e-6
NEG_INF = -1e30

LANES = 128
VMEM_LIMIT_BYTES = 56 * 1024 * 1024

CONV_HALO = 32
CONV_ROWS = 64
CONV_STRIDE = 2
TM_CONV = 512
TM_FFN = 512
TF_FFN = 256
TQ_ATT = 512

F32 = jnp.float32
BF16 = jnp.bfloat16


def _rms(x, g):
    return x * lax.rsqrt(jnp.mean(x * x, axis=-1, keepdims=True) + EPS) * g


def _const_spec(shape):
    nd = len(shape)
    return pl.BlockSpec(shape, lambda *_: (0,) * nd, pipeline_mode=pl.Buffered(1))


def _layer_spec(stacked_shape, layer):
    nd = len(stacked_shape)
    return pl.BlockSpec((None,) + tuple(stacked_shape[1:]),
                        lambda *_: (layer,) + (0,) * (nd - 1), pipeline_mode=pl.Buffered(1))


def _conv_mixer_kernel(x_ref, xp_ref, g_ref, w1_ref, b1_ref, wdw_ref, bdw_ref, lng_ref,
                       lnb_ref, w2_ref, b2_ref, o_ref, abuf_even, abuf_odd, ybuf, xn_s, z_s,
                       *, tiles_per_seq):
    i = pl.program_id(0)
    tm = x_ref.shape[0]
    n_slab = D_MODEL // LANES
    n_pair = n_slab // 2
    pw2_cols = D_MODEL // n_pair

    abufs = (abuf_even, abuf_odd)

    @pl.when(i == 0)
    def _():
        ybuf[...] = jnp.zeros(ybuf.shape, F32)

    @pl.when(lax.rem(i, tiles_per_seq) == 0)
    def _():
        for buf in abufs:
            buf[:, 0:CONV_HALO, :] = jnp.zeros((n_pair, CONV_HALO, LANES), F32)

    y = jnp.concatenate([ybuf[c] for c in range(n_slab)], axis=-1)
    mu = jnp.mean(y, axis=-1, keepdims=True)
    yc = y - mu
    var = jnp.mean(yc * yc, axis=-1, keepdims=True)
    z = yc * lax.rsqrt(var + EPS) * lng_ref[...] + lnb_ref[...]
    z_s[...] = (z * jax.nn.sigmoid(z)).astype(BF16)

    xn_s[...] = _rms(x_ref[...], g_ref[...]).astype(BF16)

    def glu_slab(parity, idx):
        c = 2 * idx + parity
        pre = jnp.dot(xn_s[...], w1_ref[c], preferred_element_type=F32) + b1_ref[c]
        abufs[parity][idx, CONV_HALO:CONV_HALO + tm, :] = (
            pre[:, :LANES] * jax.nn.sigmoid(pre[:, LANES:]))

    first_tap = CONV_HALO - (CONV_WIDTH - 1)
    half = CONV_ROWS // CONV_STRIDE

    def conv_slab(parity, idx):
        c = 2 * idx + parity
        src = abufs[parity]
        for r0 in range(0, tm, CONV_ROWS):
            accs = [jnp.broadcast_to(bdw_ref[c], (half, LANES))] * CONV_STRIDE
            for k in range(CONV_WIDTH):
                w = wdw_ref[c, k:k + 1, :]
                for j in range(CONV_STRIDE):
                    win = src[idx, pl.ds(r0 + j + first_tap + k, half, stride=CONV_STRIDE), :]
                    accs[j] = accs[j] + win * w
            for j in range(CONV_STRIDE):
                ybuf[c, pl.ds(r0 + j, half, stride=CONV_STRIDE), :] = accs[j]

    def pw2_piece(t):
        cs = slice(t * pw2_cols, (t + 1) * pw2_cols)
        o_ref[:, cs] = (xp_ref[:, cs] + b2_ref[:, cs]
                        + jnp.dot(z_s[...], w2_ref[:, cs], preferred_element_type=F32))

    for t in range(n_pair):
        glu_slab(0, t)
        glu_slab(1, t)
    for t in range(n_pair):
        pw2_piece(t)
    for c in range(n_slab):
        conv_slab(c % 2, c // 2)
    for buf in abufs:
        buf[:, 0:CONV_HALO, :] = buf[:, tm:tm + CONV_HALO, :]


def _pw1_slabs(w):
    lead, rows = w.shape[:-2], w.shape[-2]
    n_slab = D_MODEL // LANES
    w = w.reshape(*lead, rows, 2, n_slab, LANES)
    w = jnp.moveaxis(w, -2, -4)
    return w.reshape(*lead, n_slab, rows, 2 * LANES)


def _conv_mixer(h, layer, g, w1s_all, b1, wdw, bdw, lng, lnb, w2_all, b2):
    B, S, D = h.shape
    tm = TM_CONV
    n_slab = D // LANES
    nt = B * S // tm
    assert S % tm == 0
    row = lambda v: v.reshape(1, -1)
    b1s = _pw1_slabs(b1.reshape(1, 2 * D))
    wdws = wdw.reshape(CONV_WIDTH, n_slab, LANES).transpose(1, 0, 2)
    bdws = bdw.reshape(n_slab, 1, LANES)
    h2d = h.reshape(B * S, D)
    prev_tile = pl.BlockSpec((tm, D), lambda i: (jnp.maximum(i - 1, 0), 0))
    out = pl.pallas_call(
        functools.partial(_conv_mixer_kernel, tiles_per_seq=S // tm),
        grid=(nt + 1,),
        in_specs=[
            pl.BlockSpec((tm, D), lambda i: (jnp.minimum(i, nt - 1), 0)),
            prev_tile,
            _const_spec((1, D)),
            _layer_spec(w1s_all.shape, layer),
            _const_spec(b1s.shape),
            _const_spec(wdws.shape),
            _const_spec(bdws.shape),
            _const_spec((1, D)),
            _const_spec((1, D)),
            _layer_spec(w2_all.shape, layer),
            _const_spec((1, D)),
        ],
        out_specs=prev_tile,
        out_shape=jax.ShapeDtypeStruct((B * S, D), F32),
        scratch_shapes=[
            pltpu.VMEM((n_slab // 2, CONV_HALO + tm, LANES), F32),
            pltpu.VMEM((n_slab // 2, CONV_HALO + tm, LANES), F32),
            pltpu.VMEM((n_slab, tm, LANES), F32),
            pltpu.VMEM((tm, D), BF16),
            pltpu.VMEM((tm, D), BF16),
        ],
        compiler_params=pltpu.CompilerParams(
            dimension_semantics=("arbitrary",),
            vmem_limit_bytes=VMEM_LIMIT_BYTES),
        name="conv_mixer",
    )(h2d, h2d, row(g), w1s_all, b1s, wdws, bdws, row(lng), row(lnb), w2_all, row(b2))
    return out.reshape(B, S, D)


def _ffn_kernel(*refs, emit_kv, final_norm):
    x_ref, g_ref, wup_ref, wd_ref = refs[:4]
    rest = refs[4:]
    x = x_ref[...]
    xn = _rms(x, g_ref[...]).astype(BF16)
    acc = x
    for c in range(D_FF // TF_FFN):
        lo = c * TF_FFN
        gate = jnp.dot(xn, wup_ref[:, lo:lo + TF_FFN], preferred_element_type=F32)
        up = jnp.dot(xn, wup_ref[:, D_FF + lo:D_FF + lo + TF_FFN], preferred_element_type=F32)
        act = (gate * jax.nn.sigmoid(gate) * up).astype(BF16)
        acc = acc + jnp.dot(act, wd_ref[lo:lo + TF_FFN, :], preferred_element_type=F32)
    if emit_kv:
        gkv_ref, wk_ref, wvt_ref, o_ref, k_ref, vt_ref = rest
        o_ref[...] = acc
        hn = _rms(acc, gkv_ref[...]).astype(BF16)
        k_ref[...] = jnp.dot(hn, wk_ref[...], preferred_element_type=F32).astype(BF16)
        vt_ref[...] = lax.dot_general(wvt_ref[...], hn, (((1,), (1,)), ((), ())),
                                      preferred_element_type=F32).astype(BF16)
    elif final_norm:
        gf_ref, o_ref = rest
        o_ref[...] = _rms(acc, gf_ref[...])
    else:
        (o_ref,) = rest
        o_ref[...] = acc


def _ffn(h2d, layer, g, w_up_all, w_down_all, *, kv=None, final_g=None):
    M, D = h2d.shape
    tm = TM_FFN
    tile = pl.BlockSpec((tm, D), lambda i: (i, 0))
    in_specs = [tile, _const_spec((1, D)), _layer_spec(w_up_all.shape, layer),
                _layer_spec(w_down_all.shape, layer)]
    args = [h2d, g.reshape(1, D), w_up_all, w_down_all]
    out_specs = tile
    out_shape = jax.ShapeDtypeStruct((M, D), F32)
    if kv is not None:
        g_kv, w_k, w_vt = kv
        in_specs += [_const_spec((1, D)), _const_spec(w_k.shape), _const_spec(w_vt.shape)]
        args += [g_kv.reshape(1, D), w_k, w_vt]
        nk, nv = w_k.shape[1], w_vt.shape[0]
        out_specs = [tile, pl.BlockSpec((tm, nk), lambda i: (i, 0)),
                     pl.BlockSpec((nv, tm), lambda i: (0, i))]
        out_shape = [out_shape, jax.ShapeDtypeStruct((M, nk), BF16),
                     jax.ShapeDtypeStruct((nv, M), BF16)]
    elif final_g is not None:
        in_specs += [_const_spec((1, D))]
        args += [final_g.reshape(1, D)]
    return pl.pallas_call(
        functools.partial(_ffn_kernel, emit_kv=kv is not None,
                          final_norm=final_g is not None),
        grid=(M // tm,),
        in_specs=in_specs,
        out_specs=out_specs,
        out_shape=out_shape,
        compiler_params=pltpu.CompilerParams(
            dimension_semantics=("arbitrary",),
            vmem_limit_bytes=VMEM_LIMIT_BYTES),
        name="ffn",
    )(*args)


def _band_bucket_table():
    qi = jnp.arange(BLOCK, dtype=jnp.int32)
    kj = jnp.arange(2 * BLOCK, dtype=jnp.int32)
    dist = qi[:, None] + BLOCK - kj[None, :]
    in_window = (dist >= 0) & (dist < WINDOW)
    max_exact = N_BUCKETS // 2
    d = jnp.maximum(dist, 0)
    log_ratio = jnp.log(jnp.maximum(d, 1).astype(F32) / max_exact) / math.log(MAX_DISTANCE / max_exact)
    large = max_exact + (log_ratio * (N_BUCKETS - max_exact)).astype(jnp.int32)
    large = jnp.minimum(large, N_BUCKETS - 1)
    bucket = jnp.where(d < max_exact, d, large)
    inner = jnp.where(in_window, bucket, -1)
    first = jnp.where(in_window & (kj[None, :] >= BLOCK), bucket, -1)
    return jnp.stack([inner.T, first.T])


def _bias_table_kernel(rb_ref, bucket_ref, o_ref):
    h = pl.program_id(0)
    bucket = bucket_ref[...]
    acc = jnp.full(bucket.shape, NEG_INF, F32)
    for j in range(N_BUCKETS):
        acc = jnp.where(bucket == j, rb_ref[j, h], acc)
    o_ref[:, 0] = acc


def _bias_table(rel_bias):
    bucket = _band_bucket_table()
    return pl.pallas_call(
        _bias_table_kernel,
        grid=(N_HEADS,),
        in_specs=[pl.BlockSpec(memory_space=pltpu.SMEM),
                  pl.BlockSpec(bucket.shape, lambda h: (0, 0, 0))],
        out_specs=pl.BlockSpec((2, 1, 2 * BLOCK, BLOCK), lambda h: (0, h // GROUP, 0, h % GROUP)),
        out_shape=jax.ShapeDtypeStruct((2, N_KV_HEADS, 2 * BLOCK, GROUP * BLOCK), F32),
        name="bias_table",
    )(rel_bias, bucket)


def _attn_mixer_kernel(sink_ref, x_ref, xp_ref, g_ref, wqt_ref, kc_ref, kp_ref, vtc_ref, vtp_ref,
                       bias_ref, wo_ref, o_ref, qt_buf, attnt_buf, attn_prev, *, tiles_per_seq):
    i = pl.program_id(0)
    tq = x_ref.shape[0]

    @pl.when(i == 0)
    def _():
        attnt_buf[...] = jnp.zeros(attnt_buf.shape, BF16)

    xn = _rms(x_ref[...], g_ref[...]).astype(BF16)
    qt = lax.dot_general(wqt_ref[...], xn, (((1,), (1,)), ((), ())),
                         preferred_element_type=F32)
    qt_buf[...] = (qt * (HEAD_DIM ** -0.5)).astype(BF16)

    attn_prev[...] = attnt_buf[...].T
    first_tile = lax.rem(i, tiles_per_seq) == 0

    n_qb = tq // BLOCK
    group_heads = [[kvh * GROUP + i for i in range(GROUP)] for kvh in range(N_KV_HEADS)]

    def scores(qb):
        ql = slice(qb * BLOCK, (qb + 1) * BLOCK)
        sel = jnp.where(first_tile, 1, 0) if qb == 0 else 0
        out = []
        for kvh, heads in enumerate(group_heads):
            kl = slice(kvh * LANES, kvh * LANES + HEAD_DIM)
            if qb == 0:
                k_band = jnp.concatenate([kp_ref[:, kl], kc_ref[0:BLOCK, kl]], axis=0)
            else:
                k_band = kc_ref[(qb - 1) * BLOCK:(qb + 1) * BLOCK, kl]
            q4 = jnp.concatenate([qt_buf[h * HEAD_DIM:(h + 1) * HEAD_DIM, ql] for h in heads],
                                 axis=1)
            out.append(jnp.dot(k_band, q4, preferred_element_type=F32) + bias_ref[sel, kvh])
        return out

    def attend(qb, sts):
        ql = slice(qb * BLOCK, (qb + 1) * BLOCK)
        for kvh, heads in enumerate(group_heads):
            vr = slice(kvh * HEAD_DIM, (kvh + 1) * HEAD_DIM)
            if qb == 0:
                vt_band = jnp.concatenate([vtp_ref[vr, :], vtc_ref[vr, 0:BLOCK]], axis=1)
            else:
                vt_band = vtc_ref[vr, (qb - 1) * BLOCK:(qb + 1) * BLOCK]
            st = sts[kvh]
            sink = jnp.concatenate([jnp.full((1, BLOCK), sink_ref[h], F32) for h in heads], axis=1)
            m = jnp.maximum(jnp.max(st, axis=0, keepdims=True), sink)
            pexp = jnp.exp(st - m)
            denom = jnp.sum(pexp, axis=0, keepdims=True) + jnp.exp(sink - m)
            ot = jnp.dot(vt_band, pexp.astype(BF16), preferred_element_type=F32) / denom
            for i, h in enumerate(heads):
                attnt_buf[h * HEAD_DIM:(h + 1) * HEAD_DIM, ql] = (
                    ot[:, i * BLOCK:(i + 1) * BLOCK].astype(BF16))

    def out_proj_piece(t):
        cs = slice(t * (D_MODEL // n_qb), (t + 1) * (D_MODEL // n_qb))
        o_ref[:, cs] = xp_ref[:, cs] + jnp.dot(attn_prev[...], wo_ref[:, cs],
                                               preferred_element_type=F32)

    sts = scores(0)
    for qb in range(n_qb):
        nxt = scores(qb + 1) if qb + 1 < n_qb else None
        out_proj_piece(qb)
        attend(qb, sts)
        sts = nxt


def _attn_mixer(h2d, seq_len, layer, g, wqt_all, k2d, vt2d, bias, sinks, wo_all):
    M, D = h2d.shape
    tq = TQ_ATT
    per = tq // BLOCK
    nk, nv = k2d.shape[1], vt2d.shape[0]
    assert seq_len % tq == 0
    nt = M // tq
    cur = lambda i: jnp.minimum(i, nt - 1)
    win = lambda i: jnp.maximum(cur(i) * per - 1, 0)
    prev_tile = pl.BlockSpec((tq, D), lambda i: (jnp.maximum(i - 1, 0), 0))
    return pl.pallas_call(
        functools.partial(_attn_mixer_kernel, tiles_per_seq=seq_len // tq),
        grid=(nt + 1,),
        in_specs=[
            pl.BlockSpec(memory_space=pltpu.SMEM),
            pl.BlockSpec((tq, D), lambda i: (cur(i), 0)),
            prev_tile,
            _const_spec((1, D)),
            _layer_spec(wqt_all.shape, layer),
            pl.BlockSpec((tq, nk), lambda i: (cur(i), 0)),
            pl.BlockSpec((BLOCK, nk), lambda i: (win(i), 0)),
            pl.BlockSpec((nv, tq), lambda i: (0, cur(i))),
            pl.BlockSpec((nv, BLOCK), lambda i: (0, win(i))),
            _const_spec(bias.shape),
            _layer_spec(wo_all.shape, layer),
        ],
        out_specs=prev_tile,
        out_shape=jax.ShapeDtypeStruct((M, D), F32),
        scratch_shapes=[pltpu.VMEM((D, tq), BF16), pltpu.VMEM((D, tq), BF16),
                        pltpu.VMEM((tq, D), BF16)],
        compiler_params=pltpu.CompilerParams(
            dimension_semantics=("arbitrary",),
            vmem_limit_bytes=VMEM_LIMIT_BYTES),
        name="attn_mixer",
    )(sinks, h2d, h2d, g.reshape(1, D), wqt_all, k2d, k2d, vt2d, vt2d, bias, wo_all)


def kernel(x, norm_mix, norm_ffn, conv_w_pw1, conv_b_pw1, conv_w_dw, conv_b_dw, conv_ln_g, conv_ln_b, conv_w_pw2, conv_b_pw2, norm_kv, w_kv, w_q, w_o, sinks, rel_bias, ffn_w_up, ffn_w_down, norm_final):
    B, S, D = x.shape
    kvd = N_KV_HEADS * HEAD_DIM
    w_k = jnp.pad(w_kv[:, :kvd].reshape(D, N_KV_HEADS, HEAD_DIM),
                  ((0, 0), (0, 0), (0, LANES - HEAD_DIM))).reshape(D, N_KV_HEADS * LANES).astype(BF16)
    w_vt = w_kv[:, kvd:].T.astype(BF16)
    bias = _bias_table(rel_bias)
    w1s_all = _pw1_slabs(conv_w_pw1).astype(BF16)
    w2_all = conv_w_pw2.astype(BF16)
    wqt_all, wo_all = jnp.swapaxes(w_q, 1, 2).astype(BF16), w_o.astype(BF16)
    wup_all, wdown_all = ffn_w_up.astype(BF16), ffn_w_down.astype(BF16)
    h = x
    k2d = vt2d = None
    for l in range(DEPTH):
        if l < N_A_LAYERS:
            h = _conv_mixer(h, l, norm_mix[l], w1s_all, conv_b_pw1[l], conv_w_dw[l], conv_b_dw[l],
                            conv_ln_g[l], conv_ln_b[l], w2_all, conv_b_pw2[l])
            h2d = h.reshape(B * S, D)
        else:
            j = l - N_A_LAYERS
            h2d = _attn_mixer(h2d, S, j, norm_mix[l], wqt_all, k2d, vt2d, bias, sinks[j], wo_all)
        if l == N_A_LAYERS - 1:
            h2d, k2d, vt2d = _ffn(h2d, l, norm_ffn[l], wup_all, wdown_all, kv=(norm_kv, w_k, w_vt))
        elif l == DEPTH - 1:
            h2d = _ffn(h2d, l, norm_ffn[l], wup_all, wdown_all, final_g=norm_final)
        else:
            h2d = _ffn(h2d, l, norm_ffn[l], wup_all, wdown_all)
        h = h2d.reshape(B, S, D)
    return h
```

```python
import functools
import math

import jax
import jax.numpy as jnp
from jax import lax
from jax.experimental import pallas as pl
from jax.experimental.pallas import tpu as pltpu

D_MODEL = 1024
DEPTH = 4
N_A_LAYERS = DEPTH // 2
CONV_WIDTH = 31
HEAD_DIM = 64
N_HEADS = D_MODEL // HEAD_DIM
N_KV_HEADS = N_HEADS // 4
GROUP = N_HEADS // N_KV_HEADS
WINDOW = 128
BLOCK = 128
N_BUCKETS = 32
MAX_DISTANCE = 128
D_FF = -(-8 * D_MODEL // (3 * 256)) * 256
EPS = 1e-6
NEG_INF = -1e30
LOG2E = math.log2(math.e)

LANES = 128
VMEM_LIMIT_BYTES = 56 * 1024 * 1024

CONV_HALO = 32
CONV_ROWS = 64
CONV_STRIDE = 2
TM_CONV = 512
TM_FFN = 512
TF_FFN = 256
TQ_ATT = 512

F32 = jnp.float32
BF16 = jnp.bfloat16


def _rms(x, g):
    return x * lax.rsqrt(jnp.mean(x * x, axis=-1, keepdims=True) + EPS) * g


def _const_spec(shape):
    nd = len(shape)
    return pl.BlockSpec(shape, lambda *_: (0,) * nd, pipeline_mode=pl.Buffered(1))


def _layer_spec(stacked_shape, layer):
    nd = len(stacked_shape)
    return pl.BlockSpec((None,) + tuple(stacked_shape[1:]),
                        lambda *_: (layer,) + (0,) * (nd - 1), pipeline_mode=pl.Buffered(1))


def _conv_mixer_kernel(x_ref, xp_ref, g_ref, w1_ref, b1_ref, wdw_ref, bdw_ref, lng_ref,
                       lnb_ref, w2_ref, b2_ref, o_ref, abuf_even, abuf_odd, ybuf, xn_s, z_s,
                       *, tiles_per_seq):
    i = pl.program_id(0)
    tm = x_ref.shape[0]
    n_slab = D_MODEL // LANES
    n_pair = n_slab // 2
    pw2_cols = D_MODEL // n_pair

    abufs = (abuf_even, abuf_odd)

    @pl.when(i == 0)
    def _():
        ybuf[...] = jnp.zeros(ybuf.shape, F32)

    @pl.when(lax.rem(i, tiles_per_seq) == 0)
    def _():
        for buf in abufs:
            buf[:, 0:CONV_HALO, :] = jnp.zeros((n_pair, CONV_HALO, LANES), F32)

    y = jnp.concatenate([ybuf[c] for c in range(n_slab)], axis=-1)
    mu = jnp.mean(y, axis=-1, keepdims=True)
    yc = y - mu
    var = jnp.mean(yc * yc, axis=-1, keepdims=True)
    z = yc * lax.rsqrt(var + EPS) * lng_ref[...] + lnb_ref[...]
    z_s[...] = (z * jax.nn.sigmoid(z)).astype(BF16)

    xn_s[...] = _rms(x_ref[...], g_ref[...]).astype(BF16)

    def glu_slab(parity, idx):
        c = 2 * idx + parity
        pre = jnp.dot(xn_s[...], w1_ref[c], preferred_element_type=F32) + b1_ref[c]
        abufs[parity][idx, CONV_HALO:CONV_HALO + tm, :] = (
            pre[:, :LANES] * jax.nn.sigmoid(pre[:, LANES:]))

    first_tap = CONV_HALO - (CONV_WIDTH - 1)
    half = CONV_ROWS // CONV_STRIDE

    def conv_slab(parity, idx):
        c = 2 * idx + parity
        src = abufs[parity]
        for r0 in range(0, tm, CONV_ROWS):
            accs = [jnp.broadcast_to(bdw_ref[c], (half, LANES))] * CONV_STRIDE
            for k in range(CONV_WIDTH):
                w = wdw_ref[c, k:k + 1, :]
                for j in range(CONV_STRIDE):
                    win = src[idx, pl.ds(r0 + j + first_tap + k, half, stride=CONV_STRIDE), :]
                    accs[j] = accs[j] + win * w
            for j in range(CONV_STRIDE):
                ybuf[c, pl.ds(r0 + j, half, stride=CONV_STRIDE), :] = accs[j]

    def pw2_piece(t):
        cs = slice(t * pw2_cols, (t + 1) * pw2_cols)
        o_ref[:, cs] = (xp_ref[:, cs] + b2_ref[:, cs]
                        + jnp.dot(z_s[...], w2_ref[:, cs], preferred_element_type=F32))

    for t in range(n_pair):
        glu_slab(0, t)
        glu_slab(1, t)
    for t in range(n_pair):
        pw2_piece(t)
    for c in range(n_slab):
        conv_slab(c % 2, c // 2)
    for buf in abufs:
        buf[:, 0:CONV_HALO, :] = buf[:, tm:tm + CONV_HALO, :]


def _pw1_slabs(w):
    lead, rows = w.shape[:-2], w.shape[-2]
    n_slab = D_MODEL // LANES
    w = w.reshape(*lead, rows, 2, n_slab, LANES)
    w = jnp.moveaxis(w, -2, -4)
    return w.reshape(*lead, n_slab, rows, 2 * LANES)


def _conv_mixer(h, layer, g, w1s_all, b1, wdw, bdw, lng, lnb, w2_all, b2):
    B, S, D = h.shape
    tm = TM_CONV
    n_slab = D // LANES
    nt = B * S // tm
    assert S % tm == 0
    row = lambda v: v.reshape(1, -1)
    b1s = _pw1_slabs(b1.reshape(1, 2 * D))
    wdws = wdw.reshape(CONV_WIDTH, n_slab, LANES).transpose(1, 0, 2)
    bdws = bdw.reshape(n_slab, 1, LANES)
    h2d = h.reshape(B * S, D)
    prev_tile = pl.BlockSpec((tm, D), lambda i: (jnp.maximum(i - 1, 0), 0))
    out = pl.pallas_call(
        functools.partial(_conv_mixer_kernel, tiles_per_seq=S // tm),
        grid=(nt + 1,),
        in_specs=[
            pl.BlockSpec((tm, D), lambda i: (jnp.minimum(i, nt - 1), 0)),
            prev_tile,
            _const_spec((1, D)),
            _layer_spec(w1s_all.shape, layer),
            _const_spec(b1s.shape),
            _const_spec(wdws.shape),
            _const_spec(bdws.shape),
            _const_spec((1, D)),
            _const_spec((1, D)),
            _layer_spec(w2_all.shape, layer),
            _const_spec((1, D)),
        ],
        out_specs=prev_tile,
        out_shape=jax.ShapeDtypeStruct((B * S, D), F32),
        scratch_shapes=[
            pltpu.VMEM((n_slab // 2, CONV_HALO + tm, LANES), F32),
            pltpu.VMEM((n_slab // 2, CONV_HALO + tm, LANES), F32),
            pltpu.VMEM((n_slab, tm, LANES), F32),
            pltpu.VMEM((tm, D), BF16),
            pltpu.VMEM((tm, D), BF16),
        ],
        compiler_params=pltpu.CompilerParams(
            dimension_semantics=("arbitrary",),
            vmem_limit_bytes=VMEM_LIMIT_BYTES),
        name="conv_mixer",
    )(h2d, h2d, row(g), w1s_all, b1s, wdws, bdws, row(lng), row(lnb), w2_all, row(b2))
    return out.reshape(B, S, D)


def _ffn_kernel(*refs, emit_kv, final_norm):
    x_ref, g_ref, wup_ref, wd_ref = refs[:4]
    rest = refs[4:]
    x = x_ref[...]
    xn = _rms(x, g_ref[...]).astype(BF16)
    acc = x
    for c in range(D_FF // TF_FFN):
        lo = c * TF_FFN
        gate = jnp.dot(xn, wup_ref[:, lo:lo + TF_FFN], preferred_element_type=F32)
        up = jnp.dot(xn, wup_ref[:, D_FF + lo:D_FF + lo + TF_FFN], preferred_element_type=F32)
        act = (gate * jax.nn.sigmoid(gate) * up).astype(BF16)
        acc = acc + jnp.dot(act, wd_ref[lo:lo + TF_FFN, :], preferred_element_type=F32)
    if emit_kv:
        gkv_ref, wk_ref, wvt_ref, o_ref, k_ref, vt_ref = rest
        o_ref[...] = acc
        hn = _rms(acc, gkv_ref[...]).astype(BF16)
        k_ref[...] = jnp.dot(hn, wk_ref[...], preferred_element_type=F32).astype(BF16)
        vt_ref[...] = lax.dot_general(wvt_ref[...], hn, (((1,), (1,)), ((), ())),
                                      preferred_element_type=F32).astype(BF16)
    elif final_norm:
        gf_ref, o_ref = rest
        o_ref[...] = _rms(acc, gf_ref[...])
    else:
        (o_ref,) = rest
        o_ref[...] = acc


def _ffn(h2d, layer, g, w_up_all, w_down_all, *, kv=None, final_g=None):
    M, D = h2d.shape
    tm = TM_FFN
    tile = pl.BlockSpec((tm, D), lambda i: (i, 0))
    in_specs = [tile, _const_spec((1, D)), _layer_spec(w_up_all.shape, layer),
                _layer_spec(w_down_all.shape, layer)]
    args = [h2d, g.reshape(1, D), w_up_all, w_down_all]
    out_specs = tile
    out_shape = jax.ShapeDtypeStruct((M, D), F32)
    if kv is not None:
        g_kv, w_k, w_vt = kv
        in_specs += [_const_spec((1, D)), _const_spec(w_k.shape), _const_spec(w_vt.shape)]
        args += [g_kv.reshape(1, D), w_k, w_vt]
        nk, nv = w_k.shape[1], w_vt.shape[0]
        out_specs = [tile, pl.BlockSpec((tm, nk), lambda i: (i, 0)),
                     pl.BlockSpec((nv, tm), lambda i: (0, i))]
        out_shape = [out_shape, jax.ShapeDtypeStruct((M, nk), BF16),
                     jax.ShapeDtypeStruct((nv, M), BF16)]
    elif final_g is not None:
        in_specs += [_const_spec((1, D))]
        args += [final_g.reshape(1, D)]
    return pl.pallas_call(
        functools.partial(_ffn_kernel, emit_kv=kv is not None,
                          final_norm=final_g is not None),
        grid=(M // tm,),
        in_specs=in_specs,
        out_specs=out_specs,
        out_shape=out_shape,
        compiler_params=pltpu.CompilerParams(
            dimension_semantics=("arbitrary",),
            vmem_limit_bytes=VMEM_LIMIT_BYTES),
        name="ffn",
    )(*args)


def _band_bucket_table():
    qi = jnp.arange(BLOCK, dtype=jnp.int32)
    kj = jnp.arange(2 * BLOCK, dtype=jnp.int32)
    dist = qi[:, None] + BLOCK - kj[None, :]
    in_window = (dist >= 0) & (dist < WINDOW)
    max_exact = N_BUCKETS // 2
    d = jnp.maximum(dist, 0)
    log_ratio = jnp.log(jnp.maximum(d, 1).astype(F32) / max_exact) / math.log(MAX_DISTANCE / max_exact)
    large = max_exact + (log_ratio * (N_BUCKETS - max_exact)).astype(jnp.int32)
    large = jnp.minimum(large, N_BUCKETS - 1)
    bucket = jnp.where(d < max_exact, d, large)
    inner = jnp.where(in_window, bucket, -1)
    first = jnp.where(in_window & (kj[None, :] >= BLOCK), bucket, -1)
    return jnp.stack([inner.T, first.T])


def _bias_table_kernel(rb_ref, bucket_ref, o_ref):
    h = pl.program_id(0)
    bucket = bucket_ref[...]
    acc = jnp.full(bucket.shape, NEG_INF, F32)
    for j in range(N_BUCKETS):
        acc = jnp.where(bucket == j, rb_ref[j, h] * LOG2E, acc)
    o_ref[:, 0] = acc


def _bias_table(rel_bias):
    bucket = _band_bucket_table()
    return pl.pallas_call(
        _bias_table_kernel,
        grid=(N_HEADS,),
        in_specs=[pl.BlockSpec(memory_space=pltpu.SMEM),
                  pl.BlockSpec(bucket.shape, lambda h: (0, 0, 0))],
        out_specs=pl.BlockSpec((2, 1, 2 * BLOCK, BLOCK), lambda h: (0, h // GROUP, 0, h % GROUP)),
        out_shape=jax.ShapeDtypeStruct((2, N_KV_HEADS, 2 * BLOCK, GROUP * BLOCK), F32),
        name="bias_table",
    )(rel_bias, bucket)


def _attn_mixer_kernel(sink_ref, x_ref, xp_ref, g_ref, wqt_ref, kc_ref, kp_ref, vtc_ref, vtp_ref,
                       bias_ref, wo_ref, o_ref, qt_buf, attnt_buf, attn_prev, *, tiles_per_seq):
    i = pl.program_id(0)
    tq = x_ref.shape[0]

    @pl.when(i == 0)
    def _():
        attnt_buf[...] = jnp.zeros(attnt_buf.shape, BF16)

    xn = _rms(x_ref[...], g_ref[...]).astype(BF16)
    qt = lax.dot_general(wqt_ref[...], xn, (((1,), (1,)), ((), ())),
                         preferred_element_type=F32)
    qt_buf[...] = (qt * (HEAD_DIM ** -0.5 * LOG2E)).astype(BF16)

    attn_prev[...] = attnt_buf[...].T
    first_tile = lax.rem(i, tiles_per_seq) == 0

    n_qb = tq // BLOCK
    group_heads = [[kvh * GROUP + i for i in range(GROUP)] for kvh in range(N_KV_HEADS)]

    def scores(qb):
        ql = slice(qb * BLOCK, (qb + 1) * BLOCK)
        sel = jnp.where(first_tile, 1, 0) if qb == 0 else 0
        out = []
        for kvh, heads in enumerate(group_heads):
            kl = slice(kvh * LANES, kvh * LANES + HEAD_DIM)
            if qb == 0:
                k_band = jnp.concatenate([kp_ref[:, kl], kc_ref[0:BLOCK, kl]], axis=0)
            else:
                k_band = kc_ref[(qb - 1) * BLOCK:(qb + 1) * BLOCK, kl]
            q4 = jnp.concatenate([qt_buf[h * HEAD_DIM:(h + 1) * HEAD_DIM, ql] for h in heads],
                                 axis=1)
            out.append(jnp.dot(k_band, q4, preferred_element_type=F32) + bias_ref[sel, kvh])
        return out

    def attend(qb, sts):
        ql = slice(qb * BLOCK, (qb + 1) * BLOCK)
        for kvh, heads in enumerate(group_heads):
            vr = slice(kvh * HEAD_DIM, (kvh + 1) * HEAD_DIM)
            if qb == 0:
                vt_band = jnp.concatenate([vtp_ref[vr, :], vtc_ref[vr, 0:BLOCK]], axis=1)
            else:
                vt_band = vtc_ref[vr, (qb - 1) * BLOCK:(qb + 1) * BLOCK]
            st = sts[kvh]
            sink = jnp.concatenate([jnp.full((1, BLOCK), sink_ref[h] * LOG2E, F32) for h in heads], axis=1)
            m = jnp.maximum(jnp.max(st, axis=0, keepdims=True), sink)
            pexp = jnp.exp2(st - m)
            denom = jnp.sum(pexp, axis=0, keepdims=True) + jnp.exp2(sink - m)
            ot = jnp.dot(vt_band, pexp.astype(BF16), preferred_element_type=F32) / denom
            for i, h in enumerate(heads):
                attnt_buf[h * HEAD_DIM:(h + 1) * HEAD_DIM, ql] = (
                    ot[:, i * BLOCK:(i + 1) * BLOCK].astype(BF16))

    def out_proj_piece(t):
        cs = slice(t * (D_MODEL // n_qb), (t + 1) * (D_MODEL // n_qb))
        o_ref[:, cs] = xp_ref[:, cs] + jnp.dot(attn_prev[...], wo_ref[:, cs],
                                               preferred_element_type=F32)

    all_sts = [scores(qb) for qb in range(n_qb)]
    for qb in range(n_qb):
        out_proj_piece(qb)
    for qb in range(n_qb):
        attend(qb, all_sts[qb])


def _attn_mixer(h2d, seq_len, layer, g, wqt_all, k2d, vt2d, bias, sinks, wo_all):
    M, D = h2d.shape
    tq = TQ_ATT
    per = tq // BLOCK
    nk, nv = k2d.shape[1], vt2d.shape[0]
    assert seq_len % tq == 0
    nt = M // tq
    cur = lambda i: jnp.minimum(i, nt - 1)
    win = lambda i: jnp.maximum(cur(i) * per - 1, 0)
    prev_tile = pl.BlockSpec((tq, D), lambda i: (jnp.maximum(i - 1, 0), 0))
    return pl.pallas_call(
        functools.partial(_attn_mixer_kernel, tiles_per_seq=seq_len // tq),
        grid=(nt + 1,),
        in_specs=[
            pl.BlockSpec(memory_space=pltpu.SMEM),
            pl.BlockSpec((tq, D), lambda i: (cur(i), 0)),
            prev_tile,
            _const_spec((1, D)),
            _layer_spec(wqt_all.shape, layer),
            pl.BlockSpec((tq, nk), lambda i: (cur(i), 0)),
            pl.BlockSpec((BLOCK, nk), lambda i: (win(i), 0)),
            pl.BlockSpec((nv, tq), lambda i: (0, cur(i))),
            pl.BlockSpec((nv, BLOCK), lambda i: (0, win(i))),
            _const_spec(bias.shape),
            _layer_spec(wo_all.shape, layer),
        ],
        out_specs=prev_tile,
        out_shape=jax.ShapeDtypeStruct((M, D), F32),
        scratch_shapes=[pltpu.VMEM((D, tq), BF16), pltpu.VMEM((D, tq), BF16),
                        pltpu.VMEM((tq, D), BF16)],
        compiler_params=pltpu.CompilerParams(
            dimension_semantics=("arbitrary",),
            vmem_limit_bytes=VMEM_LIMIT_BYTES),
        name="attn_mixer",
    )(sinks, h2d, h2d, g.reshape(1, D), wqt_all, k2d, k2d, vt2d, vt2d, bias, wo_all)


def kernel(x, norm_mix, norm_ffn, conv_w_pw1, conv_b_pw1, conv_w_dw, conv_b_dw, conv_ln_g, conv_ln_b, conv_w_pw2, conv_b_pw2, norm_kv, w_kv, w_q, w_o, sinks, rel_bias, ffn_w_up, ffn_w_down, norm_final):
    B, S, D = x.shape
    kvd = N_KV_HEADS * HEAD_DIM
    w_k = jnp.pad(w_kv[:, :kvd].reshape(D, N_KV_HEADS, HEAD_DIM),
                  ((0, 0), (0, 0), (0, LANES - HEAD_DIM))).reshape(D, N_KV_HEADS * LANES).astype(BF16)
    w_vt = w_kv[:, kvd:].T.astype(BF16)
    bias = _bias_table(rel_bias)
    w1s_all = _pw1_slabs(conv_w_pw1).astype(BF16)
    w2_all = conv_w_pw2.astype(BF16)
    wqt_all, wo_all = jnp.swapaxes(w_q, 1, 2).astype(BF16), w_o.astype(BF16)
    wup_all, wdown_all = ffn_w_up.astype(BF16), ffn_w_down.astype(BF16)
    h = x
    k2d = vt2d = None
    for l in range(DEPTH):
        if l < N_A_LAYERS:
            h = _conv_mixer(h, l, norm_mix[l], w1s_all, conv_b_pw1[l], conv_w_dw[l], conv_b_dw[l],
                            conv_ln_g[l], conv_ln_b[l], w2_all, conv_b_pw2[l])
            h2d = h.reshape(B * S, D)
        else:
            j = l - N_A_LAYERS
            h2d = _attn_mixer(h2d, S, j, norm_mix[l], wqt_all, k2d, vt2d, bias, sinks[j], wo_all)
        if l == N_A_LAYERS - 1:
            h2d, k2d, vt2d = _ffn(h2d, l, norm_ffn[l], wup_all, wdown_all, kv=(norm_kv, w_k, w_vt))
        elif l == DEPTH - 1:
            h2d = _ffn(h2d, l, norm_ffn[l], wup_all, wdown_all, final_g=norm_final)
        else:
            h2d = _ffn(h2d, l, norm_ffn[l], wup_all, wdown_all)
        h = h2d.reshape(B, S, D)
    return h
```

```python
import functools
import math

import jax
import jax.numpy as jnp
from jax import lax
from jax.experimental import pallas as pl
from jax.experimental.pallas import tpu as pltpu

D_MODEL = 1024
DEPTH = 4
N_A_LAYERS = DEPTH // 2
CONV_WIDTH = 31
HEAD_DIM = 64
N_HEADS = D_MODEL // HEAD_DIM
N_KV_HEADS = N_HEADS // 4
GROUP = N_HEADS // N_KV_HEADS
WINDOW = 128
BLOCK = 128
N_BUCKETS = 32
MAX_DISTANCE = 128
D_FF = -(-8 * D_MODEL // (3 * 256)) * 256
EPS = 1e-6
NEG_INF = -1e30
LOG2E = math.log2(math.e)

LANES = 128
VMEM_LIMIT_BYTES = 56 * 1024 * 1024

CONV_HALO = 32
CONV_ROWS = 64
CONV_STRIDE = 2
TM_CONV = 512
TM_FFN = 1024
TF_FFN = 256
TQ_ATT = 512

F32 = jnp.float32
BF16 = jnp.bfloat16


def _rms(x, g):
    return x * lax.rsqrt(jnp.mean(x * x, axis=-1, keepdims=True) + EPS) * g


def _const_spec(shape):
    nd = len(shape)
    return pl.BlockSpec(shape, lambda *_: (0,) * nd, pipeline_mode=pl.Buffered(1))


def _layer_spec(stacked_shape, layer):
    nd = len(stacked_shape)
    return pl.BlockSpec((None,) + tuple(stacked_shape[1:]),
                        lambda *_: (layer,) + (0,) * (nd - 1), pipeline_mode=pl.Buffered(1))


def _conv_mixer_kernel(x_ref, xp_ref, g_ref, w1_ref, b1_ref, wdw_ref, bdw_ref, lng_ref,
                       lnb_ref, w2_ref, b2_ref, o_ref, abuf_even, abuf_odd, ybuf, xn_s, z_s,
                       *, tiles_per_seq):
    i = pl.program_id(0)
    tm = x_ref.shape[0]
    n_slab = D_MODEL // LANES
    n_pair = n_slab // 2
    pw2_cols = D_MODEL // n_pair

    abufs = (abuf_even, abuf_odd)

    @pl.when(i == 0)
    def _():
        ybuf[...] = jnp.zeros(ybuf.shape, F32)

    @pl.when(lax.rem(i, tiles_per_seq) == 0)
    def _():
        for buf in abufs:
            buf[:, 0:CONV_HALO, :] = jnp.zeros((n_pair, CONV_HALO, LANES), F32)

    y = jnp.concatenate([ybuf[c] for c in range(n_slab)], axis=-1)
    mu = jnp.mean(y, axis=-1, keepdims=True)
    yc = y - mu
    var = jnp.mean(yc * yc, axis=-1, keepdims=True)
    z = yc * lax.rsqrt(var + EPS) * lng_ref[...] + lnb_ref[...]
    z_s[...] = (z * jax.nn.sigmoid(z)).astype(BF16)

    xn_s[...] = _rms(x_ref[...], g_ref[...]).astype(BF16)

    def glu_slab(parity, idx):
        c = 2 * idx + parity
        pre = jnp.dot(xn_s[...], w1_ref[c], preferred_element_type=F32) + b1_ref[c]
        abufs[parity][idx, CONV_HALO:CONV_HALO + tm, :] = (
            pre[:, :LANES] * jax.nn.sigmoid(pre[:, LANES:]))

    first_tap = CONV_HALO - (CONV_WIDTH - 1)
    half = CONV_ROWS // CONV_STRIDE

    def conv_slab(parity, idx):
        c = 2 * idx + parity
        src = abufs[parity]
        for r0 in range(0, tm, CONV_ROWS):
            accs = [jnp.broadcast_to(bdw_ref[c], (half, LANES))] * CONV_STRIDE
            for k in range(CONV_WIDTH):
                w = wdw_ref[c, k:k + 1, :]
                for j in range(CONV_STRIDE):
                    win = src[idx, pl.ds(r0 + j + first_tap + k, half, stride=CONV_STRIDE), :]
                    accs[j] = accs[j] + win * w
            for j in range(CONV_STRIDE):
                ybuf[c, pl.ds(r0 + j, half, stride=CONV_STRIDE), :] = accs[j]

    def pw2_piece(t):
        cs = slice(t * pw2_cols, (t + 1) * pw2_cols)
        o_ref[:, cs] = (xp_ref[:, cs] + b2_ref[:, cs]
                        + jnp.dot(z_s[...], w2_ref[:, cs], preferred_element_type=F32))

    for t in range(n_pair):
        glu_slab(0, t)
        glu_slab(1, t)
    for t in range(n_pair):
        pw2_piece(t)
    for c in range(n_slab):
        conv_slab(c % 2, c // 2)
    for buf in abufs:
        buf[:, 0:CONV_HALO, :] = buf[:, tm:tm + CONV_HALO, :]


def _pw1_slabs(w):
    lead, rows = w.shape[:-2], w.shape[-2]
    n_slab = D_MODEL // LANES
    w = w.reshape(*lead, rows, 2, n_slab, LANES)
    w = jnp.moveaxis(w, -2, -4)
    return w.reshape(*lead, n_slab, rows, 2 * LANES)


def _conv_mixer(h, layer, g, w1s_all, b1, wdw, bdw, lng, lnb, w2_all, b2):
    B, S, D = h.shape
    tm = TM_CONV
    n_slab = D // LANES
    nt = B * S // tm
    assert S % tm == 0
    row = lambda v: v.reshape(1, -1)
    b1s = _pw1_slabs(b1.reshape(1, 2 * D))
    wdws = wdw.reshape(CONV_WIDTH, n_slab, LANES).transpose(1, 0, 2)
    bdws = bdw.reshape(n_slab, 1, LANES)
    h2d = h.reshape(B * S, D)
    prev_tile = pl.BlockSpec((tm, D), lambda i: (jnp.maximum(i - 1, 0), 0))
    out = pl.pallas_call(
        functools.partial(_conv_mixer_kernel, tiles_per_seq=S // tm),
        grid=(nt + 1,),
        in_specs=[
            pl.BlockSpec((tm, D), lambda i: (jnp.minimum(i, nt - 1), 0)),
            prev_tile,
            _const_spec((1, D)),
            _layer_spec(w1s_all.shape, layer),
            _const_spec(b1s.shape),
            _const_spec(wdws.shape),
            _const_spec(bdws.shape),
            _const_spec((1, D)),
            _const_spec((1, D)),
            _layer_spec(w2_all.shape, layer),
            _const_spec((1, D)),
        ],
        out_specs=prev_tile,
        out_shape=jax.ShapeDtypeStruct((B * S, D), F32),
        scratch_shapes=[
            pltpu.VMEM((n_slab // 2, CONV_HALO + tm, LANES), F32),
            pltpu.VMEM((n_slab // 2, CONV_HALO + tm, LANES), F32),
            pltpu.VMEM((n_slab, tm, LANES), F32),
            pltpu.VMEM((tm, D), BF16),
            pltpu.VMEM((tm, D), BF16),
        ],
        compiler_params=pltpu.CompilerParams(
            dimension_semantics=("arbitrary",),
            vmem_limit_bytes=VMEM_LIMIT_BYTES),
        name="conv_mixer",
    )(h2d, h2d, row(g), w1s_all, b1s, wdws, bdws, row(lng), row(lnb), w2_all, row(b2))
    return out.reshape(B, S, D)


def _ffn_kernel(*refs, emit_kv, final_norm):
    x_ref, g_ref, wup_ref, wd_ref = refs[:4]
    rest = refs[4:]
    x = x_ref[...]
    xn = _rms(x, g_ref[...]).astype(BF16)
    acc = x
    for c in range(D_FF // TF_FFN):
        lo = c * TF_FFN
        gate = jnp.dot(xn, wup_ref[:, lo:lo + TF_FFN], preferred_element_type=F32)
        up = jnp.dot(xn, wup_ref[:, D_FF + lo:D_FF + lo + TF_FFN], preferred_element_type=F32)
        act = (gate * jax.nn.sigmoid(gate) * up).astype(BF16)
        acc = acc + jnp.dot(act, wd_ref[lo:lo + TF_FFN, :], preferred_element_type=F32)
    if emit_kv:
        gkv_ref, wk_ref, wvt_ref, o_ref, k_ref, vt_ref = rest
        o_ref[...] = acc
        hn = _rms(acc, gkv_ref[...]).astype(BF16)
        k_ref[...] = jnp.dot(hn, wk_ref[...], preferred_element_type=F32).astype(BF16)
        vt_ref[...] = lax.dot_general(wvt_ref[...], hn, (((1,), (1,)), ((), ())),
                                      preferred_element_type=F32).astype(BF16)
    elif final_norm:
        gf_ref, o_ref = rest
        o_ref[...] = _rms(acc, gf_ref[...])
    else:
        (o_ref,) = rest
        o_ref[...] = acc


def _ffn(h2d, layer, g, w_up_all, w_down_all, *, kv=None, final_g=None):
    M, D = h2d.shape
    tm = TM_FFN
    tile = pl.BlockSpec((tm, D), lambda i: (i, 0))
    in_specs = [tile, _const_spec((1, D)), _layer_spec(w_up_all.shape, layer),
                _layer_spec(w_down_all.shape, layer)]
    args = [h2d, g.reshape(1, D), w_up_all, w_down_all]
    out_specs = tile
    out_shape = jax.ShapeDtypeStruct((M, D), F32)
    if kv is not None:
        g_kv, w_k, w_vt = kv
        in_specs += [_const_spec((1, D)), _const_spec(w_k.shape), _const_spec(w_vt.shape)]
        args += [g_kv.reshape(1, D), w_k, w_vt]
        nk, nv = w_k.shape[1], w_vt.shape[0]
        out_specs = [tile, pl.BlockSpec((tm, nk), lambda i: (i, 0)),
                     pl.BlockSpec((nv, tm), lambda i: (0, i))]
        out_shape = [out_shape, jax.ShapeDtypeStruct((M, nk), BF16),
                     jax.ShapeDtypeStruct((nv, M), BF16)]
    elif final_g is not None:
        in_specs += [_const_spec((1, D))]
        args += [final_g.reshape(1, D)]
    return pl.pallas_call(
        functools.partial(_ffn_kernel, emit_kv=kv is not None,
                          final_norm=final_g is not None),
        grid=(M // tm,),
        in_specs=in_specs,
        out_specs=out_specs,
        out_shape=out_shape,
        compiler_params=pltpu.CompilerParams(
            dimension_semantics=("arbitrary",),
            vmem_limit_bytes=VMEM_LIMIT_BYTES),
        name="ffn",
    )(*args)


def _band_bucket_table():
    qi = jnp.arange(BLOCK, dtype=jnp.int32)
    kj = jnp.arange(2 * BLOCK, dtype=jnp.int32)
    dist = qi[:, None] + BLOCK - kj[None, :]
    in_window = (dist >= 0) & (dist < WINDOW)
    max_exact = N_BUCKETS // 2
    d = jnp.maximum(dist, 0)
    log_ratio = jnp.log(jnp.maximum(d, 1).astype(F32) / max_exact) / math.log(MAX_DISTANCE / max_exact)
    large = max_exact + (log_ratio * (N_BUCKETS - max_exact)).astype(jnp.int32)
    large = jnp.minimum(large, N_BUCKETS - 1)
    bucket = jnp.where(d < max_exact, d, large)
    inner = jnp.where(in_window, bucket, -1)
    first = jnp.where(in_window & (kj[None, :] >= BLOCK), bucket, -1)
    return jnp.stack([inner.T, first.T])


def _bias_table_kernel(rb_ref, bucket_ref, o_ref):
    h = pl.program_id(0)
    bucket = bucket_ref[...]
    acc = jnp.full(bucket.shape, NEG_INF, F32)
    for j in range(N_BUCKETS):
        acc = jnp.where(bucket == j, rb_ref[j, h] * LOG2E, acc)
    o_ref[:, 0] = acc


def _bias_table(rel_bias):
    bucket = _band_bucket_table()
    return pl.pallas_call(
        _bias_table_kernel,
        grid=(N_HEADS,),
        in_specs=[pl.BlockSpec(memory_space=pltpu.SMEM),
                  pl.BlockSpec(bucket.shape, lambda h: (0, 0, 0))],
        out_specs=pl.BlockSpec((2, 1, 2 * BLOCK, BLOCK), lambda h: (0, h // GROUP, 0, h % GROUP)),
        out_shape=jax.ShapeDtypeStruct((2, N_KV_HEADS, 2 * BLOCK, GROUP * BLOCK), F32),
        name="bias_table",
    )(rel_bias, bucket)


def _attn_mixer_kernel(sink_ref, x_ref, xp_ref, g_ref, wqt_ref, kc_ref, kp_ref, vtc_ref, vtp_ref,
                       bias_ref, wo_ref, o_ref, qt_buf, attnt_buf, attn_prev, *, tiles_per_seq):
    i = pl.program_id(0)
    tq = x_ref.shape[0]

    @pl.when(i == 0)
    def _():
        attnt_buf[...] = jnp.zeros(attnt_buf.shape, BF16)

    xn = _rms(x_ref[...], g_ref[...]).astype(BF16)
    qt = lax.dot_general(wqt_ref[...], xn, (((1,), (1,)), ((), ())),
                         preferred_element_type=F32)
    qt_buf[...] = (qt * (HEAD_DIM ** -0.5 * LOG2E)).astype(BF16)

    attn_prev[...] = attnt_buf[...].T
    first_tile = lax.rem(i, tiles_per_seq) == 0

    n_qb = tq // BLOCK
    group_heads = [[kvh * GROUP + i for i in range(GROUP)] for kvh in range(N_KV_HEADS)]

    def scores(qb):
        ql = slice(qb * BLOCK, (qb + 1) * BLOCK)
        sel = jnp.where(first_tile, 1, 0) if qb == 0 else 0
        out = []
        for kvh, heads in enumerate(group_heads):
            kl = slice(kvh * HEAD_DIM, (kvh + 1) * HEAD_DIM)
            if qb == 0:
                k_band = jnp.concatenate([kp_ref[:, kl], kc_ref[0:BLOCK, kl]], axis=0)
            else:
                k_band = kc_ref[(qb - 1) * BLOCK:(qb + 1) * BLOCK, kl]
            q4 = jnp.concatenate([qt_buf[h * HEAD_DIM:(h + 1) * HEAD_DIM, ql] for h in heads],
                                 axis=1)
            out.append(jnp.dot(k_band, q4, preferred_element_type=F32) + bias_ref[sel, kvh])
        return out

    def attend(qb, sts):
        ql = slice(qb * BLOCK, (qb + 1) * BLOCK)
        for kvh, heads in enumerate(group_heads):
            vr = slice(kvh * HEAD_DIM, (kvh + 1) * HEAD_DIM)
            if qb == 0:
                vt_band = jnp.concatenate([vtp_ref[vr, :], vtc_ref[vr, 0:BLOCK]], axis=1)
            else:
                vt_band = vtc_ref[vr, (qb - 1) * BLOCK:(qb + 1) * BLOCK]
            st = sts[kvh]
            sink = jnp.concatenate([jnp.full((1, BLOCK), sink_ref[h] * LOG2E, F32) for h in heads], axis=1)
            m = jnp.maximum(jnp.max(st, axis=0, keepdims=True), sink)
            pexp = jnp.exp2(st - m)
            denom = jnp.sum(pexp, axis=0, keepdims=True) + jnp.exp2(sink - m)
            ot = jnp.dot(vt_band, pexp.astype(BF16), preferred_element_type=F32) / denom
            for i, h in enumerate(heads):
                attnt_buf[h * HEAD_DIM:(h + 1) * HEAD_DIM, ql] = (
                    ot[:, i * BLOCK:(i + 1) * BLOCK].astype(BF16))

    def out_proj_piece(t):
        cs = slice(t * (D_MODEL // n_qb), (t + 1) * (D_MODEL // n_qb))
        o_ref[:, cs] = xp_ref[:, cs] + jnp.dot(attn_prev[...], wo_ref[:, cs],
                                               preferred_element_type=F32)

    all_sts = [scores(qb) for qb in range(n_qb)]
    for qb in range(n_qb):
        out_proj_piece(qb)
    for qb in range(n_qb):
        attend(qb, all_sts[qb])


def _attn_mixer(h2d, seq_len, layer, g, wqt_all, k2d, vt2d, bias, sinks, wo_all):
    M, D = h2d.shape
    tq = TQ_ATT
    per = tq // BLOCK
    nk, nv = k2d.shape[1], vt2d.shape[0]
    assert seq_len % tq == 0
    nt = M // tq
    cur = lambda i: jnp.minimum(i, nt - 1)
    win = lambda i: jnp.maximum(cur(i) * per - 1, 0)
    prev_tile = pl.BlockSpec((tq, D), lambda i: (jnp.maximum(i - 1, 0), 0))
    return pl.pallas_call(
        functools.partial(_attn_mixer_kernel, tiles_per_seq=seq_len // tq),
        grid=(nt + 1,),
        in_specs=[
            pl.BlockSpec(memory_space=pltpu.SMEM),
            pl.BlockSpec((tq, D), lambda i: (cur(i), 0)),
            prev_tile,
            _const_spec((1, D)),
            _layer_spec(wqt_all.shape, layer),
            pl.BlockSpec((tq, nk), lambda i: (cur(i), 0)),
            pl.BlockSpec((BLOCK, nk), lambda i: (win(i), 0)),
            pl.BlockSpec((nv, tq), lambda i: (0, cur(i))),
            pl.BlockSpec((nv, BLOCK), lambda i: (0, win(i))),
            _const_spec(bias.shape),
            _layer_spec(wo_all.shape, layer),
        ],
        out_specs=prev_tile,
        out_shape=jax.ShapeDtypeStruct((M, D), F32),
        scratch_shapes=[pltpu.VMEM((D, tq), BF16), pltpu.VMEM((D, tq), BF16),
                        pltpu.VMEM((tq, D), BF16)],
        compiler_params=pltpu.CompilerParams(
            dimension_semantics=("arbitrary",),
            vmem_limit_bytes=VMEM_LIMIT_BYTES),
        name="attn_mixer",
    )(sinks, h2d, h2d, g.reshape(1, D), wqt_all, k2d, k2d, vt2d, vt2d, bias, wo_all)


def kernel(x, norm_mix, norm_ffn, conv_w_pw1, conv_b_pw1, conv_w_dw, conv_b_dw, conv_ln_g, conv_ln_b, conv_w_pw2, conv_b_pw2, norm_kv, w_kv, w_q, w_o, sinks, rel_bias, ffn_w_up, ffn_w_down, norm_final):
    B, S, D = x.shape
    kvd = N_KV_HEADS * HEAD_DIM
    w_k = w_kv[:, :kvd].astype(BF16)
    w_vt = w_kv[:, kvd:].T.astype(BF16)
    bias = _bias_table(rel_bias)
    w1s_all = _pw1_slabs(conv_w_pw1).astype(BF16)
    w2_all = conv_w_pw2.astype(BF16)
    wqt_all, wo_all = jnp.swapaxes(w_q, 1, 2).astype(BF16), w_o.astype(BF16)
    wup_all, wdown_all = ffn_w_up.astype(BF16), ffn_w_down.astype(BF16)
    h = x
    k2d = vt2d = None
    for l in range(DEPTH):
        if l < N_A_LAYERS:
            h = _conv_mixer(h, l, norm_mix[l], w1s_all, conv_b_pw1[l], conv_w_dw[l], conv_b_dw[l],
                            conv_ln_g[l], conv_ln_b[l], w2_all, conv_b_pw2[l])
            h2d = h.reshape(B * S, D)
        else:
            j = l - N_A_LAYERS
            h2d = _attn_mixer(h2d, S, j, norm_mix[l], wqt_all, k2d, vt2d, bias, sinks[j], wo_all)
        if l == N_A_LAYERS - 1:
            h2d, k2d, vt2d = _ffn(h2d, l, norm_ffn[l], wup_all, wdown_all, kv=(norm_kv, w_k, w_vt))
        elif l == DEPTH - 1:
            h2d = _ffn(h2d, l, norm_ffn[l], wup_all, wdown_all, final_g=norm_final)
        else:
            h2d = _ffn(h2d, l, norm_ffn[l], wup_all, wdown_all)
        h = h2d.reshape(B, S, D)
    return h
```

```python
import functools
import math

import jax
import jax.numpy as jnp
from jax import lax
from jax.experimental import pallas as pl
from jax.experimental.pallas import tpu as pltpu

D_MODEL = 1024
DEPTH = 4
N_A_LAYERS = DEPTH // 2
CONV_WIDTH = 31
HEAD_DIM = 64
N_HEADS = D_MODEL // HEAD_DIM
N_KV_HEADS = N_HEADS // 4
GROUP = N_HEADS // N_KV_HEADS
WINDOW = 128
BLOCK = 128
N_BUCKETS = 32
MAX_DISTANCE = 128
D_FF = -(-8 * D_MODEL // (3 * 256)) * 256
EPS = 1e-6
NEG_INF = -1e30
LOG2E = math.log2(math.e)

LANES = 128
VMEM_LIMIT_BYTES = 56 * 1024 * 1024

CONV_HALO = 32
CONV_ROWS = 64
CONV_STRIDE = 2
TM_CONV = 512
TM_FFN = 1024
TF_FFN = 256
TQ_ATT = 512

F32 = jnp.float32
BF16 = jnp.bfloat16


def _rms(x, g):
    return x * lax.rsqrt(jnp.mean(x * x, axis=-1, keepdims=True) + EPS) * g


def _const_spec(shape):
    nd = len(shape)
    return pl.BlockSpec(shape, lambda *_: (0,) * nd, pipeline_mode=pl.Buffered(1))


def _layer_spec(stacked_shape, layer):
    nd = len(stacked_shape)
    return pl.BlockSpec((None,) + tuple(stacked_shape[1:]),
                        lambda *_: (layer,) + (0,) * (nd - 1), pipeline_mode=pl.Buffered(1))


def _conv_mixer_kernel(x_ref, xp_ref, g_ref, w1_ref, b1_ref, wdw_ref, bdw_ref, lng_ref,
                       lnb_ref, w2_ref, b2_ref, o_ref, abuf_even, abuf_odd, ybuf, xn_s, z_s,
                       *, tiles_per_seq):
    i = pl.program_id(0)
    tm = x_ref.shape[0]
    n_slab = D_MODEL // LANES
    n_pair = n_slab // 2
    pw2_cols = D_MODEL // n_pair

    abufs = (abuf_even, abuf_odd)

    @pl.when(i == 0)
    def _():
        ybuf[...] = jnp.zeros(ybuf.shape, F32)

    @pl.when(lax.rem(i, tiles_per_seq) == 0)
    def _():
        for buf in abufs:
            buf[:, 0:CONV_HALO, :] = jnp.zeros((n_pair, CONV_HALO, LANES), F32)

    y = jnp.concatenate([ybuf[c] for c in range(n_slab)], axis=-1)
    mu = jnp.mean(y, axis=-1, keepdims=True)
    yc = y - mu
    var = jnp.mean(yc * yc, axis=-1, keepdims=True)
    z = yc * lax.rsqrt(var + EPS) * lng_ref[...] + lnb_ref[...]
    z_s[...] = (z * jax.nn.sigmoid(z)).astype(BF16)

    xn_s[...] = _rms(x_ref[...], g_ref[...]).astype(BF16)

    def glu_pair(idx):
        vc = slice(2 * idx * LANES, 2 * (idx + 1) * LANES)
        gc = slice(D_MODEL + 2 * idx * LANES, D_MODEL + 2 * (idx + 1) * LANES)
        xn = xn_s[...]
        val = jnp.dot(xn, w1_ref[:, vc], preferred_element_type=F32) + b1_ref[:, vc]
        gate = jnp.dot(xn, w1_ref[:, gc], preferred_element_type=F32) + b1_ref[:, gc]
        a = val * jax.nn.sigmoid(gate)
        for parity in range(2):
            abufs[parity][idx, CONV_HALO:CONV_HALO + tm, :] = a[:, parity * LANES:(parity + 1) * LANES]

    first_tap = CONV_HALO - (CONV_WIDTH - 1)
    half = CONV_ROWS // CONV_STRIDE

    def conv_slab(parity, idx):
        c = 2 * idx + parity
        src = abufs[parity]
        for r0 in range(0, tm, CONV_ROWS):
            accs = [jnp.broadcast_to(bdw_ref[c], (half, LANES))] * CONV_STRIDE
            for k in range(CONV_WIDTH):
                w = wdw_ref[c, k:k + 1, :]
                for j in range(CONV_STRIDE):
                    win = src[idx, pl.ds(r0 + j + first_tap + k, half, stride=CONV_STRIDE), :]
                    accs[j] = accs[j] + win * w
            for j in range(CONV_STRIDE):
                ybuf[c, pl.ds(r0 + j, half, stride=CONV_STRIDE), :] = accs[j]

    def pw2_piece(t):
        cs = slice(t * pw2_cols, (t + 1) * pw2_cols)
        o_ref[:, cs] = (xp_ref[:, cs] + b2_ref[:, cs]
                        + jnp.dot(z_s[...], w2_ref[:, cs], preferred_element_type=F32))

    for t in range(n_pair):
        glu_pair(t)
    for t in range(n_pair):
        pw2_piece(t)
    for c in range(n_slab):
        conv_slab(c % 2, c // 2)
    for buf in abufs:
        buf[:, 0:CONV_HALO, :] = buf[:, tm:tm + CONV_HALO, :]


def _conv_mixer(h, layer, g, w1_all, b1, wdw, bdw, lng, lnb, w2_all, b2):
    B, S, D = h.shape
    tm = TM_CONV
    n_slab = D // LANES
    nt = B * S // tm
    assert S % tm == 0
    row = lambda v: v.reshape(1, -1)
    wdws = wdw.reshape(CONV_WIDTH, n_slab, LANES).transpose(1, 0, 2)
    bdws = bdw.reshape(n_slab, 1, LANES)
    h2d = h.reshape(B * S, D)
    prev_tile = pl.BlockSpec((tm, D), lambda i: (jnp.maximum(i - 1, 0), 0))
    out = pl.pallas_call(
        functools.partial(_conv_mixer_kernel, tiles_per_seq=S // tm),
        grid=(nt + 1,),
        in_specs=[
            pl.BlockSpec((tm, D), lambda i: (jnp.minimum(i, nt - 1), 0)),
            prev_tile,
            _const_spec((1, D)),
            _layer_spec(w1_all.shape, layer),
            _const_spec((1, 2 * D)),
            _const_spec(wdws.shape),
            _const_spec(bdws.shape),
            _const_spec((1, D)),
            _const_spec((1, D)),
            _layer_spec(w2_all.shape, layer),
            _const_spec((1, D)),
        ],
        out_specs=prev_tile,
        out_shape=jax.ShapeDtypeStruct((B * S, D), F32),
        scratch_shapes=[
            pltpu.VMEM((n_slab // 2, CONV_HALO + tm, LANES), F32),
            pltpu.VMEM((n_slab // 2, CONV_HALO + tm, LANES), F32),
            pltpu.VMEM((n_slab, tm, LANES), F32),
            pltpu.VMEM((tm, D), BF16),
            pltpu.VMEM((tm, D), BF16),
        ],
        compiler_params=pltpu.CompilerParams(
            dimension_semantics=("arbitrary",),
            vmem_limit_bytes=VMEM_LIMIT_BYTES),
        name="conv_mixer",
    )(h2d, h2d, row(g), w1_all, row(b1), wdws, bdws, row(lng), row(lnb), w2_all, row(b2))
    return out.reshape(B, S, D)


def _ffn_kernel(*refs, emit_kv, final_norm):
    x_ref, g_ref, wup_ref, wd_ref = refs[:4]
    rest = refs[4:]
    x = x_ref[...]
    xn = _rms(x, g_ref[...]).astype(BF16)
    acc = x
    for c in range(D_FF // TF_FFN):
        lo = c * TF_FFN
        gate = jnp.dot(xn, wup_ref[:, lo:lo + TF_FFN], preferred_element_type=F32)
        up = jnp.dot(xn, wup_ref[:, D_FF + lo:D_FF + lo + TF_FFN], preferred_element_type=F32)
        act = (gate * jax.nn.sigmoid(gate) * up).astype(BF16)
        acc = acc + jnp.dot(act, wd_ref[lo:lo + TF_FFN, :], preferred_element_type=F32)
    if emit_kv:
        gkv_ref, wk_ref, wvt_ref, o_ref, k_ref, vt_ref = rest
        o_ref[...] = acc
        hn = _rms(acc, gkv_ref[...]).astype(BF16)
        k_ref[...] = jnp.dot(hn, wk_ref[...], preferred_element_type=F32).astype(BF16)
        vt_ref[...] = lax.dot_general(wvt_ref[...], hn, (((1,), (1,)), ((), ())),
                                      preferred_element_type=F32).astype(BF16)
    elif final_norm:
        gf_ref, o_ref = rest
        o_ref[...] = _rms(acc, gf_ref[...])
    else:
        (o_ref,) = rest
        o_ref[...] = acc


def _ffn(h2d, layer, g, w_up_all, w_down_all, *, kv=None, final_g=None):
    M, D = h2d.shape
    tm = TM_FFN
    tile = pl.BlockSpec((tm, D), lambda i: (i, 0))
    in_specs = [tile, _const_spec((1, D)), _layer_spec(w_up_all.shape, layer),
                _layer_spec(w_down_all.shape, layer)]
    args = [h2d, g.reshape(1, D), w_up_all, w_down_all]
    out_specs = tile
    out_shape = jax.ShapeDtypeStruct((M, D), F32)
    if kv is not None:
        g_kv, w_k, w_vt = kv
        in_specs += [_const_spec((1, D)), _const_spec(w_k.shape), _const_spec(w_vt.shape)]
        args += [g_kv.reshape(1, D), w_k, w_vt]
        nk, nv = w_k.shape[1], w_vt.shape[0]
        out_specs = [tile, pl.BlockSpec((tm, nk), lambda i: (i, 0)),
                     pl.BlockSpec((nv, tm), lambda i: (0, i))]
        out_shape = [out_shape, jax.ShapeDtypeStruct((M, nk), BF16),
                     jax.ShapeDtypeStruct((nv, M), BF16)]
    elif final_g is not None:
        in_specs += [_const_spec((1, D))]
        args += [final_g.reshape(1, D)]
    return pl.pallas_call(
        functools.partial(_ffn_kernel, emit_kv=kv is not None,
                          final_norm=final_g is not None),
        grid=(M // tm,),
        in_specs=in_specs,
        out_specs=out_specs,
        out_shape=out_shape,
        compiler_params=pltpu.CompilerParams(
            dimension_semantics=("arbitrary",),
            vmem_limit_bytes=VMEM_LIMIT_BYTES),
        name="ffn",
    )(*args)


def _band_bucket_table():
    qi = jnp.arange(BLOCK, dtype=jnp.int32)
    kj = jnp.arange(2 * BLOCK, dtype=jnp.int32)
    dist = qi[:, None] + BLOCK - kj[None, :]
    in_window = (dist >= 0) & (dist < WINDOW)
    max_exact = N_BUCKETS // 2
    d = jnp.maximum(dist, 0)
    log_ratio = jnp.log(jnp.maximum(d, 1).astype(F32) / max_exact) / math.log(MAX_DISTANCE / max_exact)
    large = max_exact + (log_ratio * (N_BUCKETS - max_exact)).astype(jnp.int32)
    large = jnp.minimum(large, N_BUCKETS - 1)
    bucket = jnp.where(d < max_exact, d, large)
    inner = jnp.where(in_window, bucket, -1)
    first = jnp.where(in_window & (kj[None, :] >= BLOCK), bucket, -1)
    return jnp.stack([inner.T, first.T])


def _bias_table_kernel(rb_ref, bucket_ref, o_ref):
    bucket = bucket_ref[...]
    hits = [bucket == j for j in range(N_BUCKETS)]
    for h in range(N_HEADS):
        acc = jnp.full(bucket.shape, NEG_INF, F32)
        for j in range(N_BUCKETS):
            acc = jnp.where(hits[j], rb_ref[j, h] * LOG2E, acc)
        o_ref[:, h // GROUP, :, (h % GROUP) * BLOCK:(h % GROUP + 1) * BLOCK] = acc


def _bias_table(rel_bias):
    bucket = _band_bucket_table()
    return pl.pallas_call(
        _bias_table_kernel,
        in_specs=[pl.BlockSpec(memory_space=pltpu.SMEM),
                  pl.BlockSpec(memory_space=pltpu.VMEM)],
        out_specs=pl.BlockSpec(memory_space=pltpu.VMEM),
        out_shape=jax.ShapeDtypeStruct((2, N_KV_HEADS, 2 * BLOCK, GROUP * BLOCK), F32),
        name="bias_table",
    )(rel_bias, bucket)


def _attn_mixer_kernel(sink_ref, x_ref, xp_ref, g_ref, wqt_ref, kc_ref, kp_ref, vtc_ref, vtp_ref,
                       bias_ref, wo_ref, o_ref, qt_buf, attnt_buf, attn_prev, *, tiles_per_seq):
    i = pl.program_id(0)
    tq = x_ref.shape[0]

    @pl.when(i == 0)
    def _():
        attnt_buf[...] = jnp.zeros(attnt_buf.shape, BF16)

    xn = _rms(x_ref[...], g_ref[...]).astype(BF16)
    qt = lax.dot_general(wqt_ref[...], xn, (((1,), (1,)), ((), ())),
                         preferred_element_type=F32)
    qt_buf[...] = (qt * (HEAD_DIM ** -0.5 * LOG2E)).astype(BF16)

    attn_prev[...] = attnt_buf[...].T
    first_tile = lax.rem(i, tiles_per_seq) == 0

    n_qb = tq // BLOCK
    group_heads = [[kvh * GROUP + i for i in range(GROUP)] for kvh in range(N_KV_HEADS)]

    def scores(qb):
        ql = slice(qb * BLOCK, (qb + 1) * BLOCK)
        sel = jnp.where(first_tile, 1, 0) if qb == 0 else 0
        out = []
        for kvh, heads in enumerate(group_heads):
            kl = slice(kvh * HEAD_DIM, (kvh + 1) * HEAD_DIM)
            if qb == 0:
                k_band = jnp.concatenate([kp_ref[:, kl], kc_ref[0:BLOCK, kl]], axis=0)
            else:
                k_band = kc_ref[(qb - 1) * BLOCK:(qb + 1) * BLOCK, kl]
            q4 = jnp.concatenate([qt_buf[h * HEAD_DIM:(h + 1) * HEAD_DIM, ql] for h in heads],
                                 axis=1)
            out.append(jnp.dot(k_band, q4, preferred_element_type=F32) + bias_ref[sel, kvh])
        return out

    def attend(qb, sts):
        ql = slice(qb * BLOCK, (qb + 1) * BLOCK)
        for kvh, heads in enumerate(group_heads):
            vr = slice(kvh * HEAD_DIM, (kvh + 1) * HEAD_DIM)
            if qb == 0:
                vt_band = jnp.concatenate([vtp_ref[vr, :], vtc_ref[vr, 0:BLOCK]], axis=1)
            else:
                vt_band = vtc_ref[vr, (qb - 1) * BLOCK:(qb + 1) * BLOCK]
            st = sts[kvh]
            sink = jnp.concatenate([jnp.full((1, BLOCK), sink_ref[h] * LOG2E, F32) for h in heads], axis=1)
            m = jnp.maximum(jnp.max(st, axis=0, keepdims=True), sink)
            pexp = jnp.exp2(st - m)
            denom = jnp.sum(pexp, axis=0, keepdims=True) + jnp.exp2(sink - m)
            ot = jnp.dot(vt_band, pexp.astype(BF16), preferred_element_type=F32) / denom
            for i, h in enumerate(heads):
                attnt_buf[h * HEAD_DIM:(h + 1) * HEAD_DIM, ql] = (
                    ot[:, i * BLOCK:(i + 1) * BLOCK].astype(BF16))

    def out_proj_piece(t):
        cs = slice(t * (D_MODEL // n_qb), (t + 1) * (D_MODEL // n_qb))
        o_ref[:, cs] = xp_ref[:, cs] + jnp.dot(attn_prev[...], wo_ref[:, cs],
                                               preferred_element_type=F32)

    all_sts = [scores(qb) for qb in range(n_qb)]
    for qb in range(n_qb):
        out_proj_piece(qb)
    for qb in range(n_qb):
        attend(qb, all_sts[qb])


def _attn_mixer(h2d, seq_len, layer, g, wqt_all, k2d, vt2d, bias, sinks, wo_all):
    M, D = h2d.shape
    tq = TQ_ATT
    per = tq // BLOCK
    nk, nv = k2d.shape[1], vt2d.shape[0]
    assert seq_len % tq == 0
    nt = M // tq
    cur = lambda i: jnp.minimum(i, nt - 1)
    win = lambda i: jnp.maximum(cur(i) * per - 1, 0)
    prev_tile = pl.BlockSpec((tq, D), lambda i: (jnp.maximum(i - 1, 0), 0))
    return pl.pallas_call(
        functools.partial(_attn_mixer_kernel, tiles_per_seq=seq_len // tq),
        grid=(nt + 1,),
        in_specs=[
            pl.BlockSpec(memory_space=pltpu.SMEM),
            pl.BlockSpec((tq, D), lambda i: (cur(i), 0)),
            prev_tile,
            _const_spec((1, D)),
            _layer_spec(wqt_all.shape, layer),
            pl.BlockSpec((tq, nk), lambda i: (cur(i), 0)),
            pl.BlockSpec((BLOCK, nk), lambda i: (win(i), 0)),
            pl.BlockSpec((nv, tq), lambda i: (0, cur(i))),
            pl.BlockSpec((nv, BLOCK), lambda i: (0, win(i))),
            _const_spec(bias.shape),
            _layer_spec(wo_all.shape, layer),
        ],
        out_specs=prev_tile,
        out_shape=jax.ShapeDtypeStruct((M, D), F32),
        scratch_shapes=[pltpu.VMEM((D, tq), BF16), pltpu.VMEM((D, tq), BF16),
                        pltpu.VMEM((tq, D), BF16)],
        compiler_params=pltpu.CompilerParams(
            dimension_semantics=("arbitrary",),
            vmem_limit_bytes=VMEM_LIMIT_BYTES),
        name="attn_mixer",
    )(sinks, h2d, h2d, g.reshape(1, D), wqt_all, k2d, k2d, vt2d, vt2d, bias, wo_all)


def kernel(x, norm_mix, norm_ffn, conv_w_pw1, conv_b_pw1, conv_w_dw, conv_b_dw, conv_ln_g, conv_ln_b, conv_w_pw2, conv_b_pw2, norm_kv, w_kv, w_q, w_o, sinks, rel_bias, ffn_w_up, ffn_w_down, norm_final):
    B, S, D = x.shape
    kvd = N_KV_HEADS * HEAD_DIM
    w_k = w_kv[:, :kvd].astype(BF16)
    w_vt = w_kv[:, kvd:].T.astype(BF16)
    bias = _bias_table(rel_bias)
    w1_all = conv_w_pw1.astype(BF16)
    w2_all = conv_w_pw2.astype(BF16)
    wqt_all, wo_all = jnp.swapaxes(w_q, 1, 2).astype(BF16), w_o.astype(BF16)
    wup_all, wdown_all = ffn_w_up.astype(BF16), ffn_w_down.astype(BF16)
    h = x
    k2d = vt2d = None
    for l in range(DEPTH):
        if l < N_A_LAYERS:
            h = _conv_mixer(h, l, norm_mix[l], w1_all, conv_b_pw1[l], conv_w_dw[l], conv_b_dw[l],
                            conv_ln_g[l], conv_ln_b[l], w2_all, conv_b_pw2[l])
            h2d = h.reshape(B * S, D)
        else:
            j = l - N_A_LAYERS
            h2d = _attn_mixer(h2d, S, j, norm_mix[l], wqt_all, k2d, vt2d, bias, sinks[j], wo_all)
        if l == N_A_LAYERS - 1:
            h2d, k2d, vt2d = _ffn(h2d, l, norm_ffn[l], wup_all, wdown_all, kv=(norm_kv, w_k, w_vt))
        elif l == DEPTH - 1:
            h2d = _ffn(h2d, l, norm_ffn[l], wup_all, wdown_all, final_g=norm_final)
        else:
            h2d = _ffn(h2d, l, norm_ffn[l], wup_all, wdown_all)
        h = h2d.reshape(B, S, D)
    return h
```

```python
import functools
import math

import jax
import jax.numpy as jnp
from jax import lax
from jax.experimental import pallas as pl
from jax.experimental.pallas import tpu as pltpu

D_MODEL = 1024
DEPTH = 4
N_A_LAYERS = DEPTH // 2
CONV_WIDTH = 31
HEAD_DIM = 64
N_HEADS = D_MODEL // HEAD_DIM
N_KV_HEADS = N_HEADS // 4
GROUP = N_HEADS // N_KV_HEADS
WINDOW = 128
BLOCK = 128
N_BUCKETS = 32
MAX_DISTANCE = 128
D_FF = -(-8 * D_MODEL // (3 * 256)) * 256
EPS = 1e-6
NEG_INF = -1e30
LOG2E = math.log2(math.e)

LANES = 128
VMEM_LIMIT_BYTES = 56 * 1024 * 1024

CONV_HALO = 32
CONV_ROWS = 64
CONV_STRIDE = 2
TM_CONV = 512
TM_FFN = 1024
TF_FFN = 256
TQ_ATT = 512

F32 = jnp.float32
BF16 = jnp.bfloat16


def _rms(x, g):
    return x * lax.rsqrt(jnp.mean(x * x, axis=-1, keepdims=True) + EPS) * g


def _const_spec(shape):
    nd = len(shape)
    return pl.BlockSpec(shape, lambda *_: (0,) * nd, pipeline_mode=pl.Buffered(1))


def _layer_spec(stacked_shape, layer):
    nd = len(stacked_shape)
    return pl.BlockSpec((None,) + tuple(stacked_shape[1:]),
                        lambda *_: (layer,) + (0,) * (nd - 1), pipeline_mode=pl.Buffered(1))


def _conv_mixer_kernel(x_ref, xp_ref, g_ref, w1_ref, b1_ref, wdw_ref, bdw_ref, lng_ref,
                       lnb_ref, w2_ref, b2_ref, o_ref, abuf_even, abuf_odd, ybuf, xn_s, z_s,
                       *, tiles_per_seq):
    i = pl.program_id(0)
    tm = x_ref.shape[0]
    n_slab = D_MODEL // LANES
    n_pair = n_slab // 2
    pw2_cols = D_MODEL // n_pair

    abufs = (abuf_even, abuf_odd)

    @pl.when(i == 0)
    def _():
        ybuf[...] = jnp.zeros(ybuf.shape, F32)

    @pl.when(lax.rem(i, tiles_per_seq) == 0)
    def _():
        for buf in abufs:
            buf[:, 0:CONV_HALO, :] = jnp.zeros((n_pair, CONV_HALO, LANES), F32)

    y = jnp.concatenate([ybuf[c] for c in range(n_slab)], axis=-1)
    mu = jnp.mean(y, axis=-1, keepdims=True)
    yc = y - mu
    var = jnp.mean(yc * yc, axis=-1, keepdims=True)
    z = yc * lax.rsqrt(var + EPS) * lng_ref[...] + lnb_ref[...]
    z_s[...] = (z * jax.nn.sigmoid(z)).astype(BF16)

    xn_s[...] = _rms(x_ref[...], g_ref[...]).astype(BF16)

    def glu_pair(idx):
        vc = slice(2 * idx * LANES, 2 * (idx + 1) * LANES)
        gc = slice(D_MODEL + 2 * idx * LANES, D_MODEL + 2 * (idx + 1) * LANES)
        xn = xn_s[...]
        val = jnp.dot(xn, w1_ref[:, vc], preferred_element_type=F32) + b1_ref[:, vc]
        gate = jnp.dot(xn, w1_ref[:, gc], preferred_element_type=F32) + b1_ref[:, gc]
        a = val * jax.nn.sigmoid(gate)
        for parity in range(2):
            abufs[parity][idx, CONV_HALO:CONV_HALO + tm, :] = a[:, parity * LANES:(parity + 1) * LANES]

    first_tap = CONV_HALO - (CONV_WIDTH - 1)
    half = CONV_ROWS // CONV_STRIDE

    def conv_slab(parity, idx):
        c = 2 * idx + parity
        src = abufs[parity]
        for r0 in range(0, tm, CONV_ROWS):
            accs = [jnp.broadcast_to(bdw_ref[c], (half, LANES))] * CONV_STRIDE
            for k in range(CONV_WIDTH):
                w = wdw_ref[c, k:k + 1, :]
                for j in range(CONV_STRIDE):
                    win = src[idx, pl.ds(r0 + j + first_tap + k, half, stride=CONV_STRIDE), :]
                    accs[j] = accs[j] + win * w
            for j in range(CONV_STRIDE):
                ybuf[c, pl.ds(r0 + j, half, stride=CONV_STRIDE), :] = accs[j]

    def pw2_piece(t):
        cs = slice(t * pw2_cols, (t + 1) * pw2_cols)
        o_ref[:, cs] = (xp_ref[:, cs] + b2_ref[:, cs]
                        + jnp.dot(z_s[...], w2_ref[:, cs], preferred_element_type=F32))

    for t in range(n_pair):
        glu_pair(t)
    for t in range(n_pair):
        pw2_piece(t)
    for c in range(n_slab):
        conv_slab(c % 2, c // 2)
    for buf in abufs:
        buf[:, 0:CONV_HALO, :] = buf[:, tm:tm + CONV_HALO, :]


def _conv_mixer(h, layer, g, w1_all, b1, wdw, bdw, lng, lnb, w2_all, b2):
    B, S, D = h.shape
    tm = TM_CONV
    n_slab = D // LANES
    nt = B * S // tm
    assert S % tm == 0
    row = lambda v: v.reshape(1, -1)
    wdws = wdw.reshape(CONV_WIDTH, n_slab, LANES).transpose(1, 0, 2)
    bdws = bdw.reshape(n_slab, 1, LANES)
    h2d = h.reshape(B * S, D)
    prev_tile = pl.BlockSpec((tm, D), lambda i: (jnp.maximum(i - 1, 0), 0))
    out = pl.pallas_call(
        functools.partial(_conv_mixer_kernel, tiles_per_seq=S // tm),
        grid=(nt + 1,),
        in_specs=[
            pl.BlockSpec((tm, D), lambda i: (jnp.minimum(i, nt - 1), 0)),
            prev_tile,
            _const_spec((1, D)),
            _layer_spec(w1_all.shape, layer),
            _const_spec((1, 2 * D)),
            _const_spec(wdws.shape),
            _const_spec(bdws.shape),
            _const_spec((1, D)),
            _const_spec((1, D)),
            _layer_spec(w2_all.shape, layer),
            _const_spec((1, D)),
        ],
        out_specs=prev_tile,
        out_shape=jax.ShapeDtypeStruct((B * S, D), F32),
        scratch_shapes=[
            pltpu.VMEM((n_slab // 2, CONV_HALO + tm, LANES), F32),
            pltpu.VMEM((n_slab // 2, CONV_HALO + tm, LANES), F32),
            pltpu.VMEM((n_slab, tm, LANES), F32),
            pltpu.VMEM((tm, D), BF16),
            pltpu.VMEM((tm, D), BF16),
        ],
        compiler_params=pltpu.CompilerParams(
            dimension_semantics=("arbitrary",),
            vmem_limit_bytes=VMEM_LIMIT_BYTES),
        name="conv_mixer",
    )(h2d, h2d, row(g), w1_all, row(b1), wdws, bdws, row(lng), row(lnb), w2_all, row(b2))
    return out.reshape(B, S, D)


def _ffn_kernel(*refs, emit_kv, final_norm):
    x_ref, g_ref, wup_ref, wd_ref = refs[:4]
    rest = refs[4:]
    x = x_ref[...]
    xn = _rms(x, g_ref[...]).astype(BF16)
    acc = x
    for c in range(D_FF // TF_FFN):
        lo = c * TF_FFN
        gate = jnp.dot(xn, wup_ref[:, lo:lo + TF_FFN], preferred_element_type=F32)
        up = jnp.dot(xn, wup_ref[:, D_FF + lo:D_FF + lo + TF_FFN], preferred_element_type=F32)
        act = (gate * jax.nn.sigmoid(gate) * up).astype(BF16)
        acc = acc + jnp.dot(act, wd_ref[lo:lo + TF_FFN, :], preferred_element_type=F32)
    if emit_kv:
        gkv_ref, wk_ref, wvt_ref, o_ref, k_ref, vt_ref = rest
        o_ref[...] = acc
        hn = _rms(acc, gkv_ref[...]).astype(BF16)
        k_ref[...] = jnp.dot(hn, wk_ref[...], preferred_element_type=F32).astype(BF16)
        vt_ref[...] = lax.dot_general(wvt_ref[...], hn, (((1,), (1,)), ((), ())),
                                      preferred_element_type=F32).astype(BF16)
    elif final_norm:
        gf_ref, o_ref = rest
        o_ref[...] = _rms(acc, gf_ref[...])
    else:
        (o_ref,) = rest
        o_ref[...] = acc


def _ffn(h2d, layer, g, w_up_all, w_down_all, *, kv=None, final_g=None):
    M, D = h2d.shape
    tm = TM_FFN
    tile = pl.BlockSpec((tm, D), lambda i: (i, 0))
    in_specs = [tile, _const_spec((1, D)), _layer_spec(w_up_all.shape, layer),
                _layer_spec(w_down_all.shape, layer)]
    args = [h2d, g.reshape(1, D), w_up_all, w_down_all]
    out_specs = tile
    out_shape = jax.ShapeDtypeStruct((M, D), F32)
    if kv is not None:
        g_kv, w_k, w_vt = kv
        in_specs += [_const_spec((1, D)), _const_spec(w_k.shape), _const_spec(w_vt.shape)]
        args += [g_kv.reshape(1, D), w_k, w_vt]
        nk, nv = w_k.shape[1], w_vt.shape[0]
        out_specs = [tile, pl.BlockSpec((tm, nk), lambda i: (i, 0)),
                     pl.BlockSpec((nv, tm), lambda i: (0, i))]
        out_shape = [out_shape, jax.ShapeDtypeStruct((M, nk), BF16),
                     jax.ShapeDtypeStruct((nv, M), BF16)]
    elif final_g is not None:
        in_specs += [_const_spec((1, D))]
        args += [final_g.reshape(1, D)]
    return pl.pallas_call(
        functools.partial(_ffn_kernel, emit_kv=kv is not None,
                          final_norm=final_g is not None),
        grid=(M // tm,),
        in_specs=in_specs,
        out_specs=out_specs,
        out_shape=out_shape,
        compiler_params=pltpu.CompilerParams(
            dimension_semantics=("arbitrary",),
            vmem_limit_bytes=VMEM_LIMIT_BYTES),
        name="ffn",
    )(*args)


def _band_bucket_table():
    qi = jnp.arange(BLOCK, dtype=jnp.int32)
    kj = jnp.arange(2 * BLOCK, dtype=jnp.int32)
    dist = qi[:, None] + BLOCK - kj[None, :]
    in_window = (dist >= 0) & (dist < WINDOW)
    max_exact = N_BUCKETS // 2
    d = jnp.maximum(dist, 0)
    log_ratio = jnp.log(jnp.maximum(d, 1).astype(F32) / max_exact) / math.log(MAX_DISTANCE / max_exact)
    large = max_exact + (log_ratio * (N_BUCKETS - max_exact)).astype(jnp.int32)
    large = jnp.minimum(large, N_BUCKETS - 1)
    bucket = jnp.where(d < max_exact, d, large)
    inner = jnp.where(in_window, bucket, -1)
    first = jnp.where(in_window & (kj[None, :] >= BLOCK), bucket, -1)
    return jnp.stack([inner.T, first.T])


def _bias_table_kernel(rb_ref, bucket_ref, o_ref):
    bucket = bucket_ref[...]
    hits = [bucket == j for j in range(N_BUCKETS)]
    for h in range(N_HEADS):
        acc = jnp.full(bucket.shape, NEG_INF, F32)
        for j in range(N_BUCKETS):
            acc = jnp.where(hits[j], rb_ref[j, h] * LOG2E, acc)
        o_ref[:, h // GROUP, :, (h % GROUP) * BLOCK:(h % GROUP + 1) * BLOCK] = acc


def _bias_table(rel_bias):
    bucket = _band_bucket_table()
    return pl.pallas_call(
        _bias_table_kernel,
        in_specs=[pl.BlockSpec(memory_space=pltpu.SMEM),
                  pl.BlockSpec(memory_space=pltpu.VMEM)],
        out_specs=pl.BlockSpec(memory_space=pltpu.VMEM),
        out_shape=jax.ShapeDtypeStruct((2, N_KV_HEADS, 2 * BLOCK, GROUP * BLOCK), F32),
        name="bias_table",
    )(rel_bias, bucket)


def _attn_mixer_kernel(sink_ref, x_ref, xp_ref, g_ref, wqt_ref, kc_ref, kp_ref, vtc_ref, vtp_ref,
                       bias_ref, wo_ref, o_ref, qt_buf, attnt_buf, attn_prev, *, tiles_per_seq):
    i = pl.program_id(0)
    tq = x_ref.shape[0]

    @pl.when(i == 0)
    def _():
        attnt_buf[...] = jnp.zeros(attnt_buf.shape, BF16)

    xn = _rms(x_ref[...], g_ref[...]).astype(BF16)
    qt = lax.dot_general(wqt_ref[...], xn, (((1,), (1,)), ((), ())),
                         preferred_element_type=F32)
    qt_buf[...] = (qt * (HEAD_DIM ** -0.5 * LOG2E)).astype(BF16)

    attn_prev[...] = attnt_buf[...].T
    first_tile = lax.rem(i, tiles_per_seq) == 0

    n_qb = tq // BLOCK
    group_heads = [[kvh * GROUP + i for i in range(GROUP)] for kvh in range(N_KV_HEADS)]

    def scores(qb):
        ql = slice(qb * BLOCK, (qb + 1) * BLOCK)
        sel = jnp.where(first_tile, 1, 0) if qb == 0 else 0
        out = []
        for kvh, heads in enumerate(group_heads):
            kl = slice(kvh * HEAD_DIM, (kvh + 1) * HEAD_DIM)
            if qb == 0:
                k_band = jnp.concatenate([kp_ref[:, kl], kc_ref[0:BLOCK, kl]], axis=0)
            else:
                k_band = kc_ref[(qb - 1) * BLOCK:(qb + 1) * BLOCK, kl]
            q4 = jnp.concatenate([qt_buf[h * HEAD_DIM:(h + 1) * HEAD_DIM, ql] for h in heads],
                                 axis=1)
            out.append(jnp.dot(k_band, q4, preferred_element_type=F32) + bias_ref[sel, kvh])
        return out

    def attend(qb, sts):
        ql = slice(qb * BLOCK, (qb + 1) * BLOCK)
        for kvh, heads in enumerate(group_heads):
            vr = slice(kvh * HEAD_DIM, (kvh + 1) * HEAD_DIM)
            if qb == 0:
                vt_band = jnp.concatenate([vtp_ref[vr, :], vtc_ref[vr, 0:BLOCK]], axis=1)
            else:
                vt_band = vtc_ref[vr, (qb - 1) * BLOCK:(qb + 1) * BLOCK]
            st = sts[kvh]
            sink = jnp.concatenate([jnp.full((1, BLOCK), sink_ref[h] * LOG2E, F32) for h in heads], axis=1)
            m = jnp.maximum(jnp.max(st, axis=0, keepdims=True), sink)
            pexp = jnp.exp2(st - m)
            denom = jnp.sum(pexp, axis=0, keepdims=True) + jnp.exp2(sink - m)
            ot = jnp.dot(vt_band, pexp.astype(BF16), preferred_element_type=F32) / denom
            for i, h in enumerate(heads):
                attnt_buf[h * HEAD_DIM:(h + 1) * HEAD_DIM, ql] = (
                    ot[:, i * BLOCK:(i + 1) * BLOCK].astype(BF16))

    def out_proj_piece(t):
        cs = slice(t * (D_MODEL // n_qb), (t + 1) * (D_MODEL // n_qb))
        o_ref[:, cs] = xp_ref[:, cs] + jnp.dot(attn_prev[...], wo_ref[:, cs],
                                               preferred_element_type=F32)

    all_sts = [scores(qb) for qb in range(n_qb)]
    for qb in range(n_qb):
        out_proj_piece(qb)
    for qb in range(n_qb):
        attend(qb, all_sts[qb])


def _attn_mixer(h2d, seq_len, layer, g, wqt_all, k2d, vt2d, bias, sinks, wo_all):
    M, D = h2d.shape
    tq = TQ_ATT
    per = tq // BLOCK
    nk, nv = k2d.shape[1], vt2d.shape[0]
    assert seq_len % tq == 0
    nt = M // tq
    cur = lambda i: jnp.minimum(i, nt - 1)
    win = lambda i: jnp.maximum(cur(i) * per - 1, 0)
    prev_tile = pl.BlockSpec((tq, D), lambda i: (jnp.maximum(i - 1, 0), 0))
    return pl.pallas_call(
        functools.partial(_attn_mixer_kernel, tiles_per_seq=seq_len // tq),
        grid=(nt + 1,),
        in_specs=[
            pl.BlockSpec(memory_space=pltpu.SMEM),
            pl.BlockSpec((tq, D), lambda i: (cur(i), 0)),
            prev_tile,
            _const_spec((1, D)),
            _layer_spec(wqt_all.shape, layer),
            pl.BlockSpec((tq, nk), lambda i: (cur(i), 0)),
            pl.BlockSpec((BLOCK, nk), lambda i: (win(i), 0)),
            pl.BlockSpec((nv, tq), lambda i: (0, cur(i))),
            pl.BlockSpec((nv, BLOCK), lambda i: (0, win(i))),
            _const_spec(bias.shape),
            _layer_spec(wo_all.shape, layer),
        ],
        out_specs=prev_tile,
        out_shape=jax.ShapeDtypeStruct((M, D), F32),
        scratch_shapes=[pltpu.VMEM((D, tq), BF16), pltpu.VMEM((D, tq), BF16),
                        pltpu.VMEM((tq, D), BF16)],
        compiler_params=pltpu.CompilerParams(
            dimension_semantics=("arbitrary",),
            vmem_limit_bytes=VMEM_LIMIT_BYTES),
        name="attn_mixer",
    )(sinks, h2d, h2d, g.reshape(1, D), wqt_all, k2d, k2d, vt2d, vt2d, bias, wo_all)


def kernel(x, norm_mix, norm_ffn, conv_w_pw1, conv_b_pw1, conv_w_dw, conv_b_dw, conv_ln_g, conv_ln_b, conv_w_pw2, conv_b_pw2, norm_kv, w_kv, w_q, w_o, sinks, rel_bias, ffn_w_up, ffn_w_down, norm_final):
    B, S, D = x.shape
    kvd = N_KV_HEADS * HEAD_DIM
    w_k = w_kv[:, :kvd].astype(BF16)
    w_vt = w_kv[:, kvd:].astype(BF16).T
    bias = _bias_table(rel_bias)
    w1_all = conv_w_pw1.astype(BF16)
    w2_all = conv_w_pw2.astype(BF16)
    wqt_all, wo_all = jnp.swapaxes(w_q.astype(BF16), 1, 2), w_o.astype(BF16)
    wup_all, wdown_all = ffn_w_up.astype(BF16), ffn_w_down.astype(BF16)
    h = x
    k2d = vt2d = None
    for l in range(DEPTH):
        if l < N_A_LAYERS:
            h = _conv_mixer(h, l, norm_mix[l], w1_all, conv_b_pw1[l], conv_w_dw[l], conv_b_dw[l],
                            conv_ln_g[l], conv_ln_b[l], w2_all, conv_b_pw2[l])
            h2d = h.reshape(B * S, D)
        else:
            j = l - N_A_LAYERS
            h2d = _attn_mixer(h2d, S, j, norm_mix[l], wqt_all, k2d, vt2d, bias, sinks[j], wo_all)
        if l == N_A_LAYERS - 1:
            h2d, k2d, vt2d = _ffn(h2d, l, norm_ffn[l], wup_all, wdown_all, kv=(norm_kv, w_k, w_vt))
        elif l == DEPTH - 1:
            h2d = _ffn(h2d, l, norm_ffn[l], wup_all, wdown_all, final_g=norm_final)
        else:
            h2d = _ffn(h2d, l, norm_ffn[l], wup_all, wdown_all)
        h = h2d.reshape(B, S, D)
    return h
```

```python
import functools
import math

import jax
import jax.numpy as jnp
from jax import lax
from jax.experimental import pallas as pl
from jax.experimental.pallas import tpu as pltpu

D_MODEL = 1024
DEPTH = 4
N_A_LAYERS = DEPTH // 2
CONV_WIDTH = 31
HEAD_DIM = 64
N_HEADS = D_MODEL // HEAD_DIM
N_KV_HEADS = N_HEADS // 4
GROUP = N_HEADS // N_KV_HEADS
WINDOW = 128
BLOCK = 128
N_BUCKETS = 32
MAX_DISTANCE = 128
D_FF = -(-8 * D_MODEL // (3 * 256)) * 256
EPS = 1e-6
NEG_INF = -1e30
LOG2E = math.log2(math.e)

LANES = 128
VMEM_LIMIT_BYTES = 56 * 1024 * 1024

CONV_HALO = 32
CONV_ROWS = 64
CONV_STRIDE = 2
TM_CONV = 512
TM_FFN = 1024
TF_FFN = 256
TQ_ATT = 512

F32 = jnp.float32
BF16 = jnp.bfloat16


def _rms(x, g):
    return x * lax.rsqrt(jnp.mean(x * x, axis=-1, keepdims=True) + EPS) * g


def _const_spec(shape):
    nd = len(shape)
    return pl.BlockSpec(shape, lambda *_: (0,) * nd, pipeline_mode=pl.Buffered(1))


def _layer_spec(stacked_shape, layer):
    nd = len(stacked_shape)
    return pl.BlockSpec((None,) + tuple(stacked_shape[1:]),
                        lambda *_: (layer,) + (0,) * (nd - 1), pipeline_mode=pl.Buffered(1))


def _conv_mixer_kernel(x_ref, xp_ref, g_ref, w1_ref, b1_ref, wdw_ref, bdw_ref, lng_ref,
                       lnb_ref, w2_ref, b2_ref, o_ref, abuf_even, abuf_odd, ybuf, xn_s, z_s,
                       *, tiles_per_seq):
    i = pl.program_id(0)
    tm = x_ref.shape[0]
    n_slab = D_MODEL // LANES
    n_pair = n_slab // 2
    pw2_cols = D_MODEL // n_pair

    abufs = (abuf_even, abuf_odd)

    @pl.when(i == 0)
    def _():
        ybuf[...] = jnp.zeros(ybuf.shape, F32)

    @pl.when(lax.rem(i, tiles_per_seq) == 0)
    def _():
        for buf in abufs:
            buf[:, 0:CONV_HALO, :] = jnp.zeros((n_pair, CONV_HALO, LANES), F32)

    def norm_prev():
        y = jnp.concatenate([ybuf[c] for c in range(n_slab)], axis=-1)
        mu = jnp.mean(y, axis=-1, keepdims=True)
        yc = y - mu
        var = jnp.mean(yc * yc, axis=-1, keepdims=True)
        z = yc * lax.rsqrt(var + EPS) * lng_ref[...] + lnb_ref[...]
        z_s[...] = (z * jax.nn.sigmoid(z)).astype(BF16)

    def glu_pair(idx):
        vc = slice(2 * idx * LANES, 2 * (idx + 1) * LANES)
        gc = slice(D_MODEL + 2 * idx * LANES, D_MODEL + 2 * (idx + 1) * LANES)
        xn = xn_s[...]
        val = jnp.dot(xn, w1_ref[:, vc], preferred_element_type=F32) + b1_ref[:, vc]
        gate = jnp.dot(xn, w1_ref[:, gc], preferred_element_type=F32) + b1_ref[:, gc]
        a = val * jax.nn.sigmoid(gate)
        for parity in range(2):
            abufs[parity][idx, CONV_HALO:CONV_HALO + tm, :] = a[:, parity * LANES:(parity + 1) * LANES]

    first_tap = CONV_HALO - (CONV_WIDTH - 1)
    half = CONV_ROWS // CONV_STRIDE

    def conv_slab(parity, idx):
        c = 2 * idx + parity
        src = abufs[parity]
        for r0 in range(0, tm, CONV_ROWS):
            accs = [jnp.broadcast_to(bdw_ref[c], (half, LANES))] * CONV_STRIDE
            for k in range(CONV_WIDTH):
                w = wdw_ref[c, k:k + 1, :]
                for j in range(CONV_STRIDE):
                    win = src[idx, pl.ds(r0 + j + first_tap + k, half, stride=CONV_STRIDE), :]
                    accs[j] = accs[j] + win * w
            for j in range(CONV_STRIDE):
                ybuf[c, pl.ds(r0 + j, half, stride=CONV_STRIDE), :] = accs[j]

    def pw2_piece(t):
        cs = slice(t * pw2_cols, (t + 1) * pw2_cols)
        o_ref[:, cs] = (xp_ref[:, cs] + b2_ref[:, cs]
                        + jnp.dot(z_s[...], w2_ref[:, cs], preferred_element_type=F32))

    last = pl.num_programs(0) - 1

    @pl.when(i < last)
    def _():
        norm_prev()
        xn_s[...] = _rms(x_ref[...], g_ref[...]).astype(BF16)
        for t in range(n_pair):
            glu_pair(t)
        for t in range(n_pair):
            pw2_piece(t)
        for c in range(n_slab):
            conv_slab(c % 2, c // 2)
        for buf in abufs:
            buf[:, 0:CONV_HALO, :] = buf[:, tm:tm + CONV_HALO, :]

    @pl.when(i == last)
    def _():
        norm_prev()
        for t in range(n_pair):
            pw2_piece(t)


def _conv_mixer(h, layer, g, w1_all, b1, wdw, bdw, lng, lnb, w2_all, b2):
    B, S, D = h.shape
    tm = TM_CONV
    n_slab = D // LANES
    nt = B * S // tm
    assert S % tm == 0
    row = lambda v: v.reshape(1, -1)
    wdws = wdw.reshape(CONV_WIDTH, n_slab, LANES).transpose(1, 0, 2)
    bdws = bdw.reshape(n_slab, 1, LANES)
    h2d = h.reshape(B * S, D)
    prev_tile = pl.BlockSpec((tm, D), lambda i: (jnp.maximum(i - 1, 0), 0))
    out = pl.pallas_call(
        functools.partial(_conv_mixer_kernel, tiles_per_seq=S // tm),
        grid=(nt + 1,),
        in_specs=[
            pl.BlockSpec((tm, D), lambda i: (jnp.minimum(i, nt - 1), 0)),
            prev_tile,
            _const_spec((1, D)),
            _layer_spec(w1_all.shape, layer),
            _const_spec((1, 2 * D)),
            _const_spec(wdws.shape),
            _const_spec(bdws.shape),
            _const_spec((1, D)),
            _const_spec((1, D)),
            _layer_spec(w2_all.shape, layer),
            _const_spec((1, D)),
        ],
        out_specs=prev_tile,
        out_shape=jax.ShapeDtypeStruct((B * S, D), F32),
        scratch_shapes=[
            pltpu.VMEM((n_slab // 2, CONV_HALO + tm, LANES), F32),
            pltpu.VMEM((n_slab // 2, CONV_HALO + tm, LANES), F32),
            pltpu.VMEM((n_slab, tm, LANES), F32),
            pltpu.VMEM((tm, D), BF16),
            pltpu.VMEM((tm, D), BF16),
        ],
        compiler_params=pltpu.CompilerParams(
            dimension_semantics=("arbitrary",),
            vmem_limit_bytes=VMEM_LIMIT_BYTES),
        name="conv_mixer",
    )(h2d, h2d, row(g), w1_all, row(b1), wdws, bdws, row(lng), row(lnb), w2_all, row(b2))
    return out.reshape(B, S, D)


def _ffn_kernel(*refs, emit_kv, final_norm):
    x_ref, g_ref, wup_ref, wd_ref = refs[:4]
    rest = refs[4:]
    x = x_ref[...]
    xn = _rms(x, g_ref[...]).astype(BF16)
    acc = x
    for c in range(D_FF // TF_FFN):
        lo = c * TF_FFN
        gate = jnp.dot(xn, wup_ref[:, lo:lo + TF_FFN], preferred_element_type=F32)
        up = jnp.dot(xn, wup_ref[:, D_FF + lo:D_FF + lo + TF_FFN], preferred_element_type=F32)
        act = (gate * jax.nn.sigmoid(gate) * up).astype(BF16)
        acc = acc + jnp.dot(act, wd_ref[lo:lo + TF_FFN, :], preferred_element_type=F32)
    if emit_kv:
        gkv_ref, wk_ref, wvt_ref, o_ref, k_ref, vt_ref = rest
        o_ref[...] = acc
        hn = _rms(acc, gkv_ref[...]).astype(BF16)
        k_ref[...] = jnp.dot(hn, wk_ref[...], preferred_element_type=F32).astype(BF16)
        vt_ref[...] = lax.dot_general(wvt_ref[...], hn, (((1,), (1,)), ((), ())),
                                      preferred_element_type=F32).astype(BF16)
    elif final_norm:
        gf_ref, o_ref = rest
        o_ref[...] = _rms(acc, gf_ref[...])
    else:
        (o_ref,) = rest
        o_ref[...] = acc


def _ffn(h2d, layer, g, w_up_all, w_down_all, *, kv=None, final_g=None):
    M, D = h2d.shape
    tm = TM_FFN
    tile = pl.BlockSpec((tm, D), lambda i: (i, 0))
    in_specs = [tile, _const_spec((1, D)), _layer_spec(w_up_all.shape, layer),
                _layer_spec(w_down_all.shape, layer)]
    args = [h2d, g.reshape(1, D), w_up_all, w_down_all]
    out_specs = tile
    out_shape = jax.ShapeDtypeStruct((M, D), F32)
    if kv is not None:
        g_kv, w_k, w_vt = kv
        in_specs += [_const_spec((1, D)), _const_spec(w_k.shape), _const_spec(w_vt.shape)]
        args += [g_kv.reshape(1, D), w_k, w_vt]
        nk, nv = w_k.shape[1], w_vt.shape[0]
        out_specs = [tile, pl.BlockSpec((tm, nk), lambda i: (i, 0)),
                     pl.BlockSpec((nv, tm), lambda i: (0, i))]
        out_shape = [out_shape, jax.ShapeDtypeStruct((M, nk), BF16),
                     jax.ShapeDtypeStruct((nv, M), BF16)]
    elif final_g is not None:
        in_specs += [_const_spec((1, D))]
        args += [final_g.reshape(1, D)]
    return pl.pallas_call(
        functools.partial(_ffn_kernel, emit_kv=kv is not None,
                          final_norm=final_g is not None),
        grid=(M // tm,),
        in_specs=in_specs,
        out_specs=out_specs,
        out_shape=out_shape,
        compiler_params=pltpu.CompilerParams(
            dimension_semantics=("arbitrary",),
            vmem_limit_bytes=VMEM_LIMIT_BYTES),
        name="ffn",
    )(*args)


def _band_bucket_table():
    qi = jnp.arange(BLOCK, dtype=jnp.int32)
    kj = jnp.arange(2 * BLOCK, dtype=jnp.int32)
    dist = qi[:, None] + BLOCK - kj[None, :]
    in_window = (dist >= 0) & (dist < WINDOW)
    max_exact = N_BUCKETS // 2
    d = jnp.maximum(dist, 0)
    log_ratio = jnp.log(jnp.maximum(d, 1).astype(F32) / max_exact) / math.log(MAX_DISTANCE / max_exact)
    large = max_exact + (log_ratio * (N_BUCKETS - max_exact)).astype(jnp.int32)
    large = jnp.minimum(large, N_BUCKETS - 1)
    bucket = jnp.where(d < max_exact, d, large)
    inner = jnp.where(in_window, bucket, -1)
    first = jnp.where(in_window & (kj[None, :] >= BLOCK), bucket, -1)
    return jnp.stack([inner.T, first.T])


def _bias_table_kernel(rb_ref, bucket_ref, o_ref):
    bucket = bucket_ref[...]
    hits = [bucket == j for j in range(N_BUCKETS)]
    for h in range(N_HEADS):
        acc = jnp.full(bucket.shape, NEG_INF, F32)
        for j in range(N_BUCKETS):
            acc = jnp.where(hits[j], rb_ref[j, h] * LOG2E, acc)
        o_ref[:, h // GROUP, :, (h % GROUP) * BLOCK:(h % GROUP + 1) * BLOCK] = acc


def _bias_table(rel_bias):
    bucket = _band_bucket_table()
    return pl.pallas_call(
        _bias_table_kernel,
        in_specs=[pl.BlockSpec(memory_space=pltpu.SMEM),
                  pl.BlockSpec(memory_space=pltpu.VMEM)],
        out_specs=pl.BlockSpec(memory_space=pltpu.VMEM),
        out_shape=jax.ShapeDtypeStruct((2, N_KV_HEADS, 2 * BLOCK, GROUP * BLOCK), F32),
        name="bias_table",
    )(rel_bias, bucket)


def _attn_mixer_kernel(sink_ref, x_ref, xp_ref, g_ref, wqt_ref, kc_ref, kp_ref, vtc_ref, vtp_ref,
                       bias_ref, wo_ref, o_ref, qt_buf, attnt_buf, attn_prev, *, tiles_per_seq):
    i = pl.program_id(0)
    tq = x_ref.shape[0]

    @pl.when(i == 0)
    def _():
        attnt_buf[...] = jnp.zeros(attnt_buf.shape, BF16)

    def q_proj():
        xn = _rms(x_ref[...], g_ref[...]).astype(BF16)
        qt = lax.dot_general(wqt_ref[...], xn, (((1,), (1,)), ((), ())),
                             preferred_element_type=F32)
        qt_buf[...] = (qt * (HEAD_DIM ** -0.5 * LOG2E)).astype(BF16)

    def take_prev():
        attn_prev[...] = attnt_buf[...].T

    first_tile = lax.rem(i, tiles_per_seq) == 0

    n_qb = tq // BLOCK
    group_heads = [[kvh * GROUP + i for i in range(GROUP)] for kvh in range(N_KV_HEADS)]

    def scores(qb):
        ql = slice(qb * BLOCK, (qb + 1) * BLOCK)
        sel = jnp.where(first_tile, 1, 0) if qb == 0 else 0
        out = []
        for kvh, heads in enumerate(group_heads):
            kl = slice(kvh * HEAD_DIM, (kvh + 1) * HEAD_DIM)
            if qb == 0:
                k_band = jnp.concatenate([kp_ref[:, kl], kc_ref[0:BLOCK, kl]], axis=0)
            else:
                k_band = kc_ref[(qb - 1) * BLOCK:(qb + 1) * BLOCK, kl]
            q4 = jnp.concatenate([qt_buf[h * HEAD_DIM:(h + 1) * HEAD_DIM, ql] for h in heads],
                                 axis=1)
            out.append(jnp.dot(k_band, q4, preferred_element_type=F32) + bias_ref[sel, kvh])
        return out

    def attend(qb, sts):
        ql = slice(qb * BLOCK, (qb + 1) * BLOCK)
        for kvh, heads in enumerate(group_heads):
            vr = slice(kvh * HEAD_DIM, (kvh + 1) * HEAD_DIM)
            if qb == 0:
                vt_band = jnp.concatenate([vtp_ref[vr, :], vtc_ref[vr, 0:BLOCK]], axis=1)
            else:
                vt_band = vtc_ref[vr, (qb - 1) * BLOCK:(qb + 1) * BLOCK]
            st = sts[kvh]
            sink = jnp.concatenate([jnp.full((1, BLOCK), sink_ref[h] * LOG2E, F32) for h in heads], axis=1)
            m = jnp.maximum(jnp.max(st, axis=0, keepdims=True), sink)
            pexp = jnp.exp2(st - m)
            denom = jnp.sum(pexp, axis=0, keepdims=True) + jnp.exp2(sink - m)
            ot = jnp.dot(vt_band, pexp.astype(BF16), preferred_element_type=F32) / denom
            for i, h in enumerate(heads):
                attnt_buf[h * HEAD_DIM:(h + 1) * HEAD_DIM, ql] = (
                    ot[:, i * BLOCK:(i + 1) * BLOCK].astype(BF16))

    def out_proj_piece(t):
        cs = slice(t * (D_MODEL // n_qb), (t + 1) * (D_MODEL // n_qb))
        o_ref[:, cs] = xp_ref[:, cs] + jnp.dot(attn_prev[...], wo_ref[:, cs],
                                               preferred_element_type=F32)

    last = pl.num_programs(0) - 1

    @pl.when(i < last)
    def _():
        q_proj()
        take_prev()
        all_sts = [scores(qb) for qb in range(n_qb)]
        for qb in range(n_qb):
            out_proj_piece(qb)
        for qb in range(n_qb):
            attend(qb, all_sts[qb])

    @pl.when(i == last)
    def _():
        take_prev()
        for qb in range(n_qb):
            out_proj_piece(qb)


def _attn_mixer(h2d, seq_len, layer, g, wqt_all, k2d, vt2d, bias, sinks, wo_all):
    M, D = h2d.shape
    tq = TQ_ATT
    per = tq // BLOCK
    nk, nv = k2d.shape[1], vt2d.shape[0]
    assert seq_len % tq == 0
    nt = M // tq
    cur = lambda i: jnp.minimum(i, nt - 1)
    win = lambda i: jnp.maximum(cur(i) * per - 1, 0)
    prev_tile = pl.BlockSpec((tq, D), lambda i: (jnp.maximum(i - 1, 0), 0))
    return pl.pallas_call(
        functools.partial(_attn_mixer_kernel, tiles_per_seq=seq_len // tq),
        grid=(nt + 1,),
        in_specs=[
            pl.BlockSpec(memory_space=pltpu.SMEM),
            pl.BlockSpec((tq, D), lambda i: (cur(i), 0)),
            prev_tile,
            _const_spec((1, D)),
            _layer_spec(wqt_all.shape, layer),
            pl.BlockSpec((tq, nk), lambda i: (cur(i), 0)),
            pl.BlockSpec((BLOCK, nk), lambda i: (win(i), 0)),
            pl.BlockSpec((nv, tq), lambda i: (0, cur(i))),
            pl.BlockSpec((nv, BLOCK), lambda i: (0, win(i))),
            _const_spec(bias.shape),
            _layer_spec(wo_all.shape, layer),
        ],
        out_specs=prev_tile,
        out_shape=jax.ShapeDtypeStruct((M, D), F32),
        scratch_shapes=[pltpu.VMEM((D, tq), BF16), pltpu.VMEM((D, tq), BF16),
                        pltpu.VMEM((tq, D), BF16)],
        compiler_params=pltpu.CompilerParams(
            dimension_semantics=("arbitrary",),
            vmem_limit_bytes=VMEM_LIMIT_BYTES),
        name="attn_mixer",
    )(sinks, h2d, h2d, g.reshape(1, D), wqt_all, k2d, k2d, vt2d, vt2d, bias, wo_all)


def kernel(x, norm_mix, norm_ffn, conv_w_pw1, conv_b_pw1, conv_w_dw, conv_b_dw, conv_ln_g, conv_ln_b, conv_w_pw2, conv_b_pw2, norm_kv, w_kv, w_q, w_o, sinks, rel_bias, ffn_w_up, ffn_w_down, norm_final):
    B, S, D = x.shape
    kvd = N_KV_HEADS * HEAD_DIM
    w_k = w_kv[:, :kvd].astype(BF16)
    w_vt = w_kv[:, kvd:].astype(BF16).T
    bias = _bias_table(rel_bias)
    w1_all = conv_w_pw1.astype(BF16)
    w2_all = conv_w_pw2.astype(BF16)
    wqt_all, wo_all = jnp.swapaxes(w_q.astype(BF16), 1, 2), w_o.astype(BF16)
    wup_all, wdown_all = ffn_w_up.astype(BF16), ffn_w_down.astype(BF16)
    h = x
    k2d = vt2d = None
    for l in range(DEPTH):
        if l < N_A_LAYERS:
            h = _conv_mixer(h, l, norm_mix[l], w1_all, conv_b_pw1[l], conv_w_dw[l], conv_b_dw[l],
                            conv_ln_g[l], conv_ln_b[l], w2_all, conv_b_pw2[l])
            h2d = h.reshape(B * S, D)
        else:
            j = l - N_A_LAYERS
            h2d = _attn_mixer(h2d, S, j, norm_mix[l], wqt_all, k2d, vt2d, bias, sinks[j], wo_all)
        if l == N_A_LAYERS - 1:
            h2d, k2d, vt2d = _ffn(h2d, l, norm_ffn[l], wup_all, wdown_all, kv=(norm_kv, w_k, w_vt))
        elif l == DEPTH - 1:
            h2d = _ffn(h2d, l, norm_ffn[l], wup_all, wdown_all, final_g=norm_final)
        else:
            h2d = _ffn(h2d, l, norm_ffn[l], wup_all, wdown_all)
        h = h2d.reshape(B, S, D)
    return h
```

```python
import functools
import math

import jax
import jax.numpy as jnp
from jax import lax
from jax.experimental import pallas as pl
from jax.experimental.pallas import tpu as pltpu

D_MODEL = 1024
DEPTH = 4
N_A_LAYERS = DEPTH // 2
CONV_WIDTH = 31
HEAD_DIM = 64
N_HEADS = D_MODEL // HEAD_DIM
N_KV_HEADS = N_HEADS // 4
GROUP = N_HEADS // N_KV_HEADS
WINDOW = 128
BLOCK = 128
N_BUCKETS = 32
MAX_DISTANCE = 128
D_FF = -(-8 * D_MODEL // (3 * 256)) * 256
EPS = 1e-6
NEG_INF = -1e30
LOG2E = math.log2(math.e)

LANES = 128
VMEM_LIMIT_BYTES = 56 * 1024 * 1024

CONV_HALO = 32
CONV_ROWS = 64
CONV_STRIDE = 2
TM_CONV = 512
TM_FFN = 1024
TF_FFN = 256
TQ_ATT = 512

F32 = jnp.float32
BF16 = jnp.bfloat16


def _rms(x, g):
    return x * lax.rsqrt(jnp.mean(x * x, axis=-1, keepdims=True) + EPS) * g


def _const_spec(shape):
    nd = len(shape)
    return pl.BlockSpec(shape, lambda *_: (0,) * nd, pipeline_mode=pl.Buffered(1))


def _layer_spec(stacked_shape, layer):
    nd = len(stacked_shape)
    return pl.BlockSpec((None,) + tuple(stacked_shape[1:]),
                        lambda *_: (layer,) + (0,) * (nd - 1), pipeline_mode=pl.Buffered(1))


def _conv_mixer_kernel(x_ref, xp_ref, g_ref, w1_ref, b1_ref, wdw_ref, bdw_ref, lng_ref,
                       lnb_ref, w2_ref, b2_ref, o_ref, abuf_even, abuf_odd, ybuf, xn_s, z_s,
                       *, tiles_per_seq):
    i = pl.program_id(0)
    tm = x_ref.shape[0]
    n_slab = D_MODEL // LANES
    n_pair = n_slab // 2
    pw2_cols = D_MODEL // n_pair

    abufs = (abuf_even, abuf_odd)

    @pl.when(i == 0)
    def _():
        ybuf[...] = jnp.zeros(ybuf.shape, F32)

    @pl.when(lax.rem(i, tiles_per_seq) == 0)
    def _():
        for buf in abufs:
            buf[:, 0:CONV_HALO, :] = jnp.zeros((n_pair, CONV_HALO, LANES), F32)

    y = jnp.concatenate([ybuf[c] for c in range(n_slab)], axis=-1)
    mu = jnp.mean(y, axis=-1, keepdims=True)
    yc = y - mu
    var = jnp.mean(yc * yc, axis=-1, keepdims=True)
    z = yc * lax.rsqrt(var + EPS) * lng_ref[...] + lnb_ref[...]
    z_s[...] = (z * jax.nn.sigmoid(z)).astype(BF16)

    xn_s[...] = _rms(x_ref[...], g_ref[...]).astype(BF16)

    def glu_pair(idx):
        vc = slice(2 * idx * LANES, 2 * (idx + 1) * LANES)
        gc = slice(D_MODEL + 2 * idx * LANES, D_MODEL + 2 * (idx + 1) * LANES)
        xn = xn_s[...]
        val = jnp.dot(xn, w1_ref[:, vc], preferred_element_type=F32) + b1_ref[:, vc]
        gate = jnp.dot(xn, w1_ref[:, gc], preferred_element_type=F32) + b1_ref[:, gc]
        a = val * jax.nn.sigmoid(gate)
        for parity in range(2):
            abufs[parity][idx, CONV_HALO:CONV_HALO + tm, :] = a[:, parity * LANES:(parity + 1) * LANES]

    first_tap = CONV_HALO - (CONV_WIDTH - 1)
    half = CONV_ROWS // CONV_STRIDE

    def conv_slab(parity, idx):
        c = 2 * idx + parity
        src = abufs[parity]
        for r0 in range(0, tm, CONV_ROWS):
            accs = [jnp.broadcast_to(bdw_ref[c], (half, LANES))] * CONV_STRIDE
            for k in range(CONV_WIDTH):
                w = wdw_ref[c, k:k + 1, :]
                for j in range(CONV_STRIDE):
                    win = src[idx, pl.ds(r0 + j + first_tap + k, half, stride=CONV_STRIDE), :]
                    accs[j] = accs[j] + win * w
            for j in range(CONV_STRIDE):
                ybuf[c, pl.ds(r0 + j, half, stride=CONV_STRIDE), :] = accs[j]

    def pw2_piece(t):
        cs = slice(t * pw2_cols, (t + 1) * pw2_cols)
        o_ref[:, cs] = (xp_ref[:, cs] + b2_ref[:, cs]
                        + jnp.dot(z_s[...], w2_ref[:, cs], preferred_element_type=F32))

    for t in range(n_pair):
        glu_pair(t)
    for t in range(n_pair):
        pw2_piece(t)
    for c in range(n_slab):
        conv_slab(c % 2, c // 2)
    for buf in abufs:
        buf[:, 0:CONV_HALO, :] = buf[:, tm:tm + CONV_HALO, :]


def _conv_mixer(h, layer, g, w1_all, b1, wdw, bdw, lng, lnb, w2_all, b2):
    B, S, D = h.shape
    tm = TM_CONV
    n_slab = D // LANES
    nt = B * S // tm
    assert S % tm == 0
    row = lambda v: v.reshape(1, -1)
    wdws = wdw.reshape(CONV_WIDTH, n_slab, LANES).transpose(1, 0, 2)
    bdws = bdw.reshape(n_slab, 1, LANES)
    h2d = h.reshape(B * S, D)
    prev_tile = pl.BlockSpec((tm, D), lambda i: (jnp.maximum(i - 1, 0), 0))
    out = pl.pallas_call(
        functools.partial(_conv_mixer_kernel, tiles_per_seq=S // tm),
        grid=(nt + 1,),
        in_specs=[
            pl.BlockSpec((tm, D), lambda i: (jnp.minimum(i, nt - 1), 0)),
            prev_tile,
            _const_spec((1, D)),
            _layer_spec(w1_all.shape, layer),
            _const_spec((1, 2 * D)),
            _const_spec(wdws.shape),
            _const_spec(bdws.shape),
            _const_spec((1, D)),
            _const_spec((1, D)),
            _layer_spec(w2_all.shape, layer),
            _const_spec((1, D)),
        ],
        out_specs=prev_tile,
        out_shape=jax.ShapeDtypeStruct((B * S, D), F32),
        scratch_shapes=[
            pltpu.VMEM((n_slab // 2, CONV_HALO + tm, LANES), F32),
            pltpu.VMEM((n_slab // 2, CONV_HALO + tm, LANES), F32),
            pltpu.VMEM((n_slab, tm, LANES), F32),
            pltpu.VMEM((tm, D), BF16),
            pltpu.VMEM((tm, D), BF16),
        ],
        compiler_params=pltpu.CompilerParams(
            dimension_semantics=("arbitrary",),
            vmem_limit_bytes=VMEM_LIMIT_BYTES),
        name="conv_mixer",
    )(h2d, h2d, row(g), w1_all, row(b1), wdws, bdws, row(lng), row(lnb), w2_all, row(b2))
    return out.reshape(B, S, D)


def _ffn_kernel(*refs, emit_kv, final_norm):
    x_ref, g_ref, wup_ref, wd_ref = refs[:4]
    rest = refs[4:]
    x = x_ref[...]
    xn = _rms(x, g_ref[...]).astype(BF16)
    acc = x
    for c in range(D_FF // TF_FFN):
        lo = c * TF_FFN
        gate = jnp.dot(xn, wup_ref[:, lo:lo + TF_FFN], preferred_element_type=F32)
        up = jnp.dot(xn, wup_ref[:, D_FF + lo:D_FF + lo + TF_FFN], preferred_element_type=F32)
        act = (gate * jax.nn.sigmoid(gate) * up).astype(BF16)
        acc = acc + jnp.dot(act, wd_ref[lo:lo + TF_FFN, :], preferred_element_type=F32)
    if emit_kv:
        gkv_ref, wk_ref, wvt_ref, o_ref, k_ref, vt_ref = rest
        o_ref[...] = acc
        hn = _rms(acc, gkv_ref[...]).astype(BF16)
        k_ref[...] = jnp.dot(hn, wk_ref[...], preferred_element_type=F32).astype(BF16)
        vt_ref[...] = lax.dot_general(wvt_ref[...], hn, (((1,), (1,)), ((), ())),
                                      preferred_element_type=F32).astype(BF16)
    elif final_norm:
        gf_ref, o_ref = rest
        o_ref[...] = _rms(acc, gf_ref[...])
    else:
        (o_ref,) = rest
        o_ref[...] = acc


def _ffn(h2d, layer, g, w_up_all, w_down_all, *, kv=None, final_g=None):
    M, D = h2d.shape
    tm = TM_FFN
    tile = pl.BlockSpec((tm, D), lambda i: (i, 0))
    in_specs = [tile, _const_spec((1, D)), _layer_spec(w_up_all.shape, layer),
                _layer_spec(w_down_all.shape, layer)]
    args = [h2d, g.reshape(1, D), w_up_all, w_down_all]
    out_specs = tile
    out_shape = jax.ShapeDtypeStruct((M, D), F32)
    if kv is not None:
        g_kv, w_k, w_vt = kv
        in_specs += [_const_spec((1, D)), _const_spec(w_k.shape), _const_spec(w_vt.shape)]
        args += [g_kv.reshape(1, D), w_k, w_vt]
        nk, nv = w_k.shape[1], w_vt.shape[0]
        out_specs = [tile, pl.BlockSpec((tm, nk), lambda i: (i, 0)),
                     pl.BlockSpec((nv, tm), lambda i: (0, i))]
        out_shape = [out_shape, jax.ShapeDtypeStruct((M, nk), BF16),
                     jax.ShapeDtypeStruct((nv, M), BF16)]
    elif final_g is not None:
        in_specs += [_const_spec((1, D))]
        args += [final_g.reshape(1, D)]
    return pl.pallas_call(
        functools.partial(_ffn_kernel, emit_kv=kv is not None,
                          final_norm=final_g is not None),
        grid=(M // tm,),
        in_specs=in_specs,
        out_specs=out_specs,
        out_shape=out_shape,
        compiler_params=pltpu.CompilerParams(
            dimension_semantics=("arbitrary",),
            vmem_limit_bytes=VMEM_LIMIT_BYTES),
        name="ffn",
    )(*args)


def _band_bucket_table():
    qi = jnp.arange(BLOCK, dtype=jnp.int32)
    kj = jnp.arange(2 * BLOCK, dtype=jnp.int32)
    dist = qi[:, None] + BLOCK - kj[None, :]
    in_window = (dist >= 0) & (dist < WINDOW)
    max_exact = N_BUCKETS // 2
    d = jnp.maximum(dist, 0)
    log_ratio = jnp.log(jnp.maximum(d, 1).astype(F32) / max_exact) / math.log(MAX_DISTANCE / max_exact)
    large = max_exact + (log_ratio * (N_BUCKETS - max_exact)).astype(jnp.int32)
    large = jnp.minimum(large, N_BUCKETS - 1)
    bucket = jnp.where(d < max_exact, d, large)
    inner = jnp.where(in_window, bucket, -1)
    first = jnp.where(in_window & (kj[None, :] >= BLOCK), bucket, -1)
    return jnp.stack([inner.T, first.T])


def _bias_table_kernel(rb_ref, bucket_ref, o_ref):
    bucket = bucket_ref[...]
    hits = [bucket == j for j in range(N_BUCKETS)]
    for h in range(N_HEADS):
        acc = jnp.full(bucket.shape, NEG_INF, F32)
        for j in range(N_BUCKETS):
            acc = jnp.where(hits[j], rb_ref[j, h] * LOG2E, acc)
        o_ref[:, h // GROUP, :, (h % GROUP) * BLOCK:(h % GROUP + 1) * BLOCK] = acc


def _bias_table(rel_bias):
    bucket = _band_bucket_table()
    return pl.pallas_call(
        _bias_table_kernel,
        in_specs=[pl.BlockSpec(memory_space=pltpu.SMEM),
                  pl.BlockSpec(memory_space=pltpu.VMEM)],
        out_specs=pl.BlockSpec(memory_space=pltpu.VMEM),
        out_shape=jax.ShapeDtypeStruct((2, N_KV_HEADS, 2 * BLOCK, GROUP * BLOCK), F32),
        name="bias_table",
    )(rel_bias, bucket)


def _attn_mixer_kernel(sink_ref, x_ref, xp_ref, g_ref, wqt_ref, kc_ref, kp_ref, vtc_ref, vtp_ref,
                       bias_ref, wo_ref, o_ref, qt_buf, attnt_buf, attn_prev, *, tiles_per_seq):
    i = pl.program_id(0)
    tq = x_ref.shape[0]

    @pl.when(i == 0)
    def _():
        attnt_buf[...] = jnp.zeros(attnt_buf.shape, BF16)

    xn = _rms(x_ref[...], g_ref[...]).astype(BF16)
    qt = lax.dot_general(wqt_ref[...], xn, (((1,), (1,)), ((), ())),
                         preferred_element_type=F32)
    qt_buf[...] = (qt * (HEAD_DIM ** -0.5 * LOG2E)).astype(BF16)

    attn_prev[...] = attnt_buf[...].T
    first_tile = lax.rem(i, tiles_per_seq) == 0

    n_qb = tq // BLOCK
    group_heads = [[kvh * GROUP + i for i in range(GROUP)] for kvh in range(N_KV_HEADS)]

    def scores(qb):
        ql = slice(qb * BLOCK, (qb + 1) * BLOCK)
        sel = jnp.where(first_tile, 1, 0) if qb == 0 else 0
        out = []
        for kvh, heads in enumerate(group_heads):
            kl = slice(kvh * HEAD_DIM, (kvh + 1) * HEAD_DIM)
            if qb == 0:
                k_band = jnp.concatenate([kp_ref[:, kl], kc_ref[0:BLOCK, kl]], axis=0)
            else:
                k_band = kc_ref[(qb - 1) * BLOCK:(qb + 1) * BLOCK, kl]
            q4 = jnp.concatenate([qt_buf[h * HEAD_DIM:(h + 1) * HEAD_DIM, ql] for h in heads],
                                 axis=1)
            out.append(jnp.dot(k_band, q4, preferred_element_type=F32) + bias_ref[sel, kvh])
        return out

    def attend(qb, sts):
        ql = slice(qb * BLOCK, (qb + 1) * BLOCK)
        for kvh, heads in enumerate(group_heads):
            vr = slice(kvh * HEAD_DIM, (kvh + 1) * HEAD_DIM)
            if qb == 0:
                vt_band = jnp.concatenate([vtp_ref[vr, :], vtc_ref[vr, 0:BLOCK]], axis=1)
            else:
                vt_band = vtc_ref[vr, (qb - 1) * BLOCK:(qb + 1) * BLOCK]
            probs, denoms = [], []
            for i, h in enumerate(heads):
                st = sts[kvh][:, i * BLOCK:(i + 1) * BLOCK]
                sink = sink_ref[h] * LOG2E
                m = jnp.maximum(jnp.max(st, axis=0, keepdims=True), sink)
                pexp = jnp.exp2(st - m)
                denoms.append(jnp.sum(pexp, axis=0, keepdims=True) + jnp.exp2(sink - m))
                probs.append(pexp.astype(BF16))
            pexp = jnp.concatenate(probs, axis=1)
            denom = jnp.concatenate(denoms, axis=1)
            ot = jnp.dot(vt_band, pexp, preferred_element_type=F32) / denom
            for i, h in enumerate(heads):
                attnt_buf[h * HEAD_DIM:(h + 1) * HEAD_DIM, ql] = (
                    ot[:, i * BLOCK:(i + 1) * BLOCK].astype(BF16))

    def out_proj_piece(t):
        cs = slice(t * (D_MODEL // n_qb), (t + 1) * (D_MODEL // n_qb))
        o_ref[:, cs] = xp_ref[:, cs] + jnp.dot(attn_prev[...], wo_ref[:, cs],
                                               preferred_element_type=F32)

    all_sts = [scores(qb) for qb in range(n_qb)]
    for qb in range(n_qb):
        out_proj_piece(qb)
    for qb in range(n_qb):
        attend(qb, all_sts[qb])


def _attn_mixer(h2d, seq_len, layer, g, wqt_all, k2d, vt2d, bias, sinks, wo_all):
    M, D = h2d.shape
    tq = TQ_ATT
    per = tq // BLOCK
    nk, nv = k2d.shape[1], vt2d.shape[0]
    assert seq_len % tq == 0
    nt = M // tq
    cur = lambda i: jnp.minimum(i, nt - 1)
    win = lambda i: jnp.maximum(cur(i) * per - 1, 0)
    prev_tile = pl.BlockSpec((tq, D), lambda i: (jnp.maximum(i - 1, 0), 0))
    return pl.pallas_call(
        functools.partial(_attn_mixer_kernel, tiles_per_seq=seq_len // tq),
        grid=(nt + 1,),
        in_specs=[
            pl.BlockSpec(memory_space=pltpu.SMEM),
            pl.BlockSpec((tq, D), lambda i: (cur(i), 0)),
            prev_tile,
            _const_spec((1, D)),
            _layer_spec(wqt_all.shape, layer),
            pl.BlockSpec((tq, nk), lambda i: (cur(i), 0)),
            pl.BlockSpec((BLOCK, nk), lambda i: (win(i), 0)),
            pl.BlockSpec((nv, tq), lambda i: (0, cur(i))),
            pl.BlockSpec((nv, BLOCK), lambda i: (0, win(i))),
            _const_spec(bias.shape),
            _layer_spec(wo_all.shape, layer),
        ],
        out_specs=prev_tile,
        out_shape=jax.ShapeDtypeStruct((M, D), F32),
        scratch_shapes=[pltpu.VMEM((D, tq), BF16), pltpu.VMEM((D, tq), BF16),
                        pltpu.VMEM((tq, D), BF16)],
        compiler_params=pltpu.CompilerParams(
            dimension_semantics=("arbitrary",),
            vmem_limit_bytes=VMEM_LIMIT_BYTES),
        name="attn_mixer",
    )(sinks, h2d, h2d, g.reshape(1, D), wqt_all, k2d, k2d, vt2d, vt2d, bias, wo_all)


def kernel(x, norm_mix, norm_ffn, conv_w_pw1, conv_b_pw1, conv_w_dw, conv_b_dw, conv_ln_g, conv_ln_b, conv_w_pw2, conv_b_pw2, norm_kv, w_kv, w_q, w_o, sinks, rel_bias, ffn_w_up, ffn_w_down, norm_final):
    B, S, D = x.shape
    kvd = N_KV_HEADS * HEAD_DIM
    w_k = w_kv[:, :kvd].astype(BF16)
    w_vt = w_kv[:, kvd:].astype(BF16).T
    bias = _bias_table(rel_bias)
    w1_all = conv_w_pw1.astype(BF16)
    w2_all = conv_w_pw2.astype(BF16)
    wqt_all, wo_all = jnp.swapaxes(w_q.astype(BF16), 1, 2), w_o.astype(BF16)
    wup_all, wdown_all = ffn_w_up.astype(BF16), ffn_w_down.astype(BF16)
    h = x
    k2d = vt2d = None
    for l in range(DEPTH):
        if l < N_A_LAYERS:
            h = _conv_mixer(h, l, norm_mix[l], w1_all, conv_b_pw1[l], conv_w_dw[l], conv_b_dw[l],
                            conv_ln_g[l], conv_ln_b[l], w2_all, conv_b_pw2[l])
            h2d = h.reshape(B * S, D)
        else:
            j = l - N_A_LAYERS
            h2d = _attn_mixer(h2d, S, j, norm_mix[l], wqt_all, k2d, vt2d, bias, sinks[j], wo_all)
        if l == N_A_LAYERS - 1:
            h2d, k2d, vt2d = _ffn(h2d, l, norm_ffn[l], wup_all, wdown_all, kv=(norm_kv, w_k, w_vt))
        elif l == DEPTH - 1:
            h2d = _ffn(h2d, l, norm_ffn[l], wup_all, wdown_all, final_g=norm_final)
        else:
            h2d = _ffn(h2d, l, norm_ffn[l], wup_all, wdown_all)
        h = h2d.reshape(B, S, D)
    return h
```
